```python
import jax, jax.numpy as jnp
from jax import lax
import numpy as np

D_MODEL = 2048
BATCH = 2
SEQ = 8192
DEPTH = 4

HEAD_DIM = 128
A_HEADS = 8
A_KV_HEADS = 2
A_GROUP = A_HEADS // A_KV_HEADS
B_HEADS = 8
D_A = A_HEADS * HEAD_DIM
D_B = B_HEADS * HEAD_DIM
KV_DIM = A_KV_HEADS * HEAD_DIM
D_MIX = D_A + D_B
D_IN = 2 * D_A + 2 * KV_DIM + 4 * D_B
GRID_W = 64
AXIS_DIM = HEAD_DIM // 2
ROPE_THETA = 10000.0
Q_BLOCK = 128
DILATED_PATTERNS = ((128, 1), (512, 4), (2048, 16))
MAX_REACH = 1024
ALIBI_SLOPES = (2.0 ** -np.arange(1, B_HEADS + 1)).astype(np.float32)
SCALE = HEAD_DIM ** -0.5
EPS = 1e-6
NEG_INF = -1e30

kernel_name = 'hymba_axial_gqa_dilated_encoder'


def rms_norm(x, w):
    xf = x.astype(jnp.float32)
    y = xf * lax.rsqrt(jnp.mean(xf * xf, axis=-1, keepdims=True) + EPS)
    return (y * w.astype(jnp.float32)).astype(x.dtype)


def axial_rope_tables(T):
    rows = T // GRID_W
    row = jnp.broadcast_to(jnp.arange(rows, dtype=jnp.float32)[:, None], (rows, GRID_W)).reshape(-1)
    col = jnp.broadcast_to(jnp.arange(GRID_W, dtype=jnp.float32)[None, :], (rows, GRID_W)).reshape(-1)
    inv_freq = ROPE_THETA ** (-jnp.arange(0, AXIS_DIM, 2, dtype=jnp.float32) / AXIS_DIM)
    ang_r = row[:, None] * inv_freq[None, :]
    ang_c = col[:, None] * inv_freq[None, :]
    return (jnp.cos(ang_r), jnp.sin(ang_r), jnp.cos(ang_c), jnp.sin(ang_c))


def rotate_axis(x, cos, sin):
    x1, x2 = jnp.split(x, 2, axis=-1)
    return jnp.concatenate([x1 * cos - x2 * sin, x1 * sin + x2 * cos], axis=-1)


def apply_axial_rope(x, tables):
    cr, sr, cc, sc = tables
    xf = x.astype(jnp.float32)
    out = jnp.concatenate([rotate_axis(xf[..., :AXIS_DIM], cr, sr),
                           rotate_axis(xf[..., AXIS_DIM:], cc, sc)], axis=-1)
    return out.astype(x.dtype)


def axial_gqa(q, k, v, q_gain, k_gain, tables):
    B_, T = q.shape[0], q.shape[1]
    nb = T // Q_BLOCK
    q = apply_axial_rope(rms_norm(q, q_gain).transpose(0, 2, 1, 3), tables)
    k = apply_axial_rope(rms_norm(k, k_gain).transpose(0, 2, 1, 3), tables)
    v = v.transpose(0, 2, 1, 3)
    qb = jnp.moveaxis(q.reshape(B_, A_KV_HEADS, A_GROUP, nb, Q_BLOCK, HEAD_DIM), 3, 0)

    def one_block(qblk):
        s = jnp.einsum('bkgqd,bksd->bkgqs', qblk, k, preferred_element_type=jnp.float32) * SCALE
        p = jax.nn.softmax(s, axis=-1).astype(v.dtype)
        return jnp.einsum('bkgqs,bksd->bkgqd', p, v)

    o = lax.map(one_block, qb)
    o = jnp.moveaxis(o, 0, 3).reshape(B_, A_HEADS, T, HEAD_DIM)
    return o.transpose(0, 2, 1, 3).reshape(B_, T, D_A)


def dilated_attention(q, k, v):
    B_, T = q.shape[0], q.shape[1]
    nb = T // Q_BLOCK
    q = q.transpose(0, 2, 1, 3)
    pad = ((0, 0), (0, 0), (MAX_REACH, MAX_REACH), (0, 0))
    kp = jnp.pad(k.transpose(0, 2, 1, 3), pad)
    vp = jnp.pad(v.transpose(0, 2, 1, 3), pad)
    span = Q_BLOCK + 2 * MAX_REACH
    qb = jnp.moveaxis(q.reshape(B_, B_HEADS, nb, Q_BLOCK, HEAD_DIM), 2, 0)

    patterns = []
    for w, d in DILATED_PATTERNS:
        n = w // (2 * d)
        offs = (np.arange(-n, n + 1) * d).astype(np.int32)
        idx = np.arange(Q_BLOCK, dtype=np.int32)[:, None] + MAX_REACH + offs[None, :]
        bias = -ALIBI_SLOPES[:, None, None] * np.abs(offs).astype(np.float32)[None, None, :]
        patterns.append((offs, idx, bias))

    def one_block(args):
        blk, qblk = args
        t0 = blk * Q_BLOCK
        kc = lax.dynamic_slice_in_dim(kp, t0, span, axis=2)
        vc = lax.dynamic_slice_in_dim(vp, t0, span, axis=2)
        outs, lses = [], []
        for offs, idx, bias in patterns:
            kg = kc[:, :, idx, :]
            vg = vc[:, :, idx, :]
            pos = t0 + jnp.arange(Q_BLOCK)[:, None] + offs[None, :]
            valid = (pos >= 0) & (pos < T)
            s = jnp.einsum('bhqd,bhqjd->bhqj', qblk, kg, preferred_element_type=jnp.float32) * SCALE + bias
            s = jnp.where(valid, s, NEG_INF)
            lse = jax.nn.logsumexp(s, axis=-1, keepdims=True)
            p = jnp.exp(s - lse).astype(vg.dtype)
            outs.append(jnp.einsum('bhqj,bhqjd->bhqd', p, vg, preferred_element_type=jnp.float32))
            lses.append(lse)
        wts = jax.nn.softmax(jnp.stack(lses, axis=0), axis=0)
        o = jnp.sum(wts * jnp.stack(outs, axis=0), axis=0)
        return o.astype(qblk.dtype)

    o = lax.map(one_block, (jnp.arange(nb), qb))
    o = jnp.moveaxis(o, 0, 2).reshape(B_, B_HEADS, T, HEAD_DIM)
    return o.transpose(0, 2, 1, 3).reshape(B_, T, D_B)


def setup_inputs(seed: int = 0) -> dict:
    key = jax.random.key(seed)
    ks = jax.random.split(key, 9)
    f32 = jnp.float32
    x = jax.random.normal(ks[0], (BATCH, SEQ, D_MODEL), f32)
    norm_w = 1.0 + 0.02 * jax.random.normal(ks[1], (DEPTH, D_MODEL), f32)
    w_in = jax.random.normal(ks[2], (DEPTH, D_MODEL, D_IN), f32) * D_MODEL ** -0.5
    q_norm_a = 1.0 + 0.02 * jax.random.normal(ks[3], (DEPTH, HEAD_DIM), f32)
    k_norm_a = 1.0 + 0.02 * jax.random.normal(ks[4], (DEPTH, HEAD_DIM), f32)
    out_norm_a = 1.0 + 0.02 * jax.random.normal(ks[5], (DEPTH, D_A), f32)
    out_norm_b = 1.0 + 0.02 * jax.random.normal(ks[6], (DEPTH, D_B), f32)
    w_out = jax.random.normal(ks[7], (DEPTH, D_MIX, D_MODEL), f32) * D_MIX ** -0.5
    final_norm = 1.0 + 0.02 * jax.random.normal(ks[8], (D_MODEL,), f32)
    return {'x': x, 'norm_w': norm_w, 'w_in': w_in, 'q_norm_a': q_norm_a, 'k_norm_a': k_norm_a,
            'out_norm_a': out_norm_a, 'out_norm_b': out_norm_b, 'w_out': w_out, 'final_norm': final_norm}


def reference(x, norm_w, w_in, q_norm_a, k_norm_a, out_norm_a, out_norm_b, w_out, final_norm):
    B_, T, _ = x.shape
    tables = axial_rope_tables(T)
    splits = [int(s) for s in np.cumsum([D_A, KV_DIM, KV_DIM, D_A, D_B, D_B, D_B])]
    for l in range(DEPTH):
        h = rms_norm(x, norm_w[l])
        proj = h @ w_in[l]
        q_a, k_a, v_a, g_a, q_b, k_b, v_b, g_b = jnp.split(proj, splits, axis=-1)
        y_a = axial_gqa(q_a.reshape(B_, T, A_HEADS, HEAD_DIM),
                        k_a.reshape(B_, T, A_KV_HEADS, HEAD_DIM),
                        v_a.reshape(B_, T, A_KV_HEADS, HEAD_DIM),
                        q_norm_a[l], k_norm_a[l], tables)
        y_b = dilated_attention(q_b.reshape(B_, T, B_HEADS, HEAD_DIM),
                                k_b.reshape(B_, T, B_HEADS, HEAD_DIM),
                                v_b.reshape(B_, T, B_HEADS, HEAD_DIM))
        y = jnp.concatenate([rms_norm(y_a, out_norm_a[l]) * jax.nn.silu(g_a),
                             rms_norm(y_b, out_norm_b[l]) * jax.nn.silu(g_b)], axis=-1)
        x = x + y @ w_out[l]
    return rms_norm(x, final_norm)
```

```python
import functools

import numpy as np
import jax
import jax.numpy as jnp
from jax import lax
from jax.experimental import pallas as pl
from jax.experimental.pallas import tpu as pltpu

D_MODEL = 2048
HEAD_DIM = 128
A_HEADS = 8
A_KV_HEADS = 2
A_GROUP = A_HEADS // A_KV_HEADS
B_HEADS = 8
D_A = A_HEADS * HEAD_DIM
D_B = B_HEADS * HEAD_DIM
KV_DIM = A_KV_HEADS * HEAD_DIM
D_IN = 2 * D_A + 2 * KV_DIM + 4 * D_B
GRID_W = 64
AXIS_DIM = HEAD_DIM // 2
ROPE_THETA = 10000.0
DILATED_PATTERNS = ((128, 1), (512, 4), (2048, 16))
HALF_WIN = 64
SCALE = HEAD_DIM ** -0.5
EPS = 1e-6
NEG_INF = -1e30

COL_QA, COL_KA, COL_VA, COL_GA = 0, D_A, D_A + KV_DIM, D_A + 2 * KV_DIM
COL_QB = COL_GA + D_A
COL_KB, COL_VB, COL_GB = COL_QB + D_B, COL_QB + 2 * D_B, COL_QB + 3 * D_B

LANES = 128
VMEM_LIMIT = 56 * 1024 * 1024

BF16 = jnp.bfloat16
F32 = jnp.float32


def _cparams(sem):
    return pltpu.CompilerParams(dimension_semantics=sem, vmem_limit_bytes=VMEM_LIMIT)


def _rope_group(a, gain, cos, sin):
    ms = jnp.mean(a * a, axis=-1, keepdims=True)
    y = a * lax.rsqrt(ms + EPS) * gain
    lane = lax.broadcasted_iota(jnp.int32, y.shape, 1)
    partner = jnp.where((lane % AXIS_DIM) < AXIS_DIM // 2,
                        pltpu.roll(y, LANES - AXIS_DIM // 2, 1), pltpu.roll(y, AXIS_DIM // 2, 1))
    return y * cos + partner * sin


def _inproj_kernel(x_ref, nw_ref, w_ref, qg_ref, kg_ref, cos_ref, sin_ref, o_ref, h_ref, *, tn):
    j = pl.program_id(1)

    @pl.when(j == 0)
    def _():
        x = x_ref[...]
        ms = jnp.mean(x * x, axis=-1, keepdims=True)
        h_ref[...] = (x * lax.rsqrt(ms + EPS) * nw_ref[...]).astype(BF16)

    acc = jnp.dot(h_ref[...], w_ref[...], preferred_element_type=F32)

    groups_per_tile = tn // HEAD_DIM
    n_q, n_k = D_A // HEAD_DIM, KV_DIM // HEAD_DIM
    n_special = -(-(n_q + n_k) // groups_per_tile)

    def emit(tile):
        for c in range(groups_per_tile):
            g = tile * groups_per_tile + c
            a = acc[:, c * HEAD_DIM:(c + 1) * HEAD_DIM]
            if g < n_q:
                a = _rope_group(a, qg_ref[...], cos_ref[...], sin_ref[...])
            elif g < n_q + n_k:
                a = _rope_group(a, kg_ref[...], cos_ref[...], sin_ref[...])
            o_ref[:, c * HEAD_DIM:(c + 1) * HEAD_DIM] = a.astype(o_ref.dtype)

    for tile in range(n_special):
        pl.when(j == tile)(functools.partial(emit, tile))

    @pl.when(j >= n_special)
    def _():
        o_ref[...] = acc.astype(o_ref.dtype)


def _inproj(x2, nw, w, qg, kg, cos, sin, *, seq, tm=1024, tn=512):
    m = x2.shape[0]
    t_blocks = seq // tm
    return pl.pallas_call(
        functools.partial(_inproj_kernel, tn=tn),
        out_shape=jax.ShapeDtypeStruct((m, D_IN), BF16),
        grid=(m // tm, D_IN // tn),
        in_specs=[
            pl.BlockSpec((tm, D_MODEL), lambda i, j: (i, 0)),
            pl.BlockSpec((1, D_MODEL), lambda i, j: (0, 0)),
            pl.BlockSpec((D_MODEL, tn), lambda i, j: (0, j)),
            pl.BlockSpec((1, HEAD_DIM), lambda i, j: (0, 0)),
            pl.BlockSpec((1, HEAD_DIM), lambda i, j: (0, 0)),
            pl.BlockSpec((tm, HEAD_DIM), lambda i, j: (i % t_blocks, 0)),
            pl.BlockSpec((tm, HEAD_DIM), lambda i, j: (i % t_blocks, 0)),
        ],
        out_specs=pl.BlockSpec((tm, tn), lambda i, j: (i, j)),
        scratch_shapes=[pltpu.VMEM((tm, D_MODEL), BF16)],
        compiler_params=_cparams(("parallel", "arbitrary")),
        name="inproj",
    )(x2, nw, w, qg, kg, cos, sin)


def _attn_a_kernel(q_ref, k_ref, v_ref, o_ref, *, tk):
    tq = q_ref.shape[0]
    n_kv = k_ref.shape[0] // tk
    for hh in range(A_GROUP):
        q = q_ref[:, hh * HEAD_DIM:(hh + 1) * HEAD_DIM]

        def body(kk, carry):
            m, l, acc = carry
            start = pl.multiple_of(kk * tk, tk)
            k = k_ref[pl.ds(start, tk), :]
            v = v_ref[pl.ds(start, tk), :]
            s = lax.dot_general(q, k, (((1,), (1,)), ((), ())), preferred_element_type=F32)
            m_new = jnp.maximum(m, jnp.max(s, axis=-1, keepdims=True))
            alpha = jnp.exp(m - m_new)
            p = jnp.exp(s - m_new)
            l = alpha * l + jnp.sum(p, axis=-1, keepdims=True)
            acc = alpha * acc + jnp.dot(p.astype(BF16), v, preferred_element_type=F32)
            return m_new, l, acc

        init = (jnp.full((tq, 1), -jnp.inf, F32), jnp.zeros((tq, 1), F32),
                jnp.zeros((tq, HEAD_DIM), F32))
        m, l, acc = lax.fori_loop(0, n_kv, body, init)
        o_ref[:, hh * HEAD_DIM:(hh + 1) * HEAD_DIM] = (acc / l).astype(o_ref.dtype)


def _attn_a(proj3, *, tq=256, tk=512):
    b, t, _ = proj3.shape
    gw = A_GROUP * HEAD_DIM
    return pl.pallas_call(
        functools.partial(_attn_a_kernel, tk=tk),
        out_shape=jax.ShapeDtypeStruct((b, t, D_A), BF16),
        grid=(b, A_KV_HEADS, t // tq),
        in_specs=[
            pl.BlockSpec((None, tq, gw), lambda bi, g, qi: (bi, qi, COL_QA // gw + g)),
            pl.BlockSpec((None, t, HEAD_DIM), lambda bi, g, qi: (bi, 0, COL_KA // HEAD_DIM + g)),
            pl.BlockSpec((None, t, HEAD_DIM), lambda bi, g, qi: (bi, 0, COL_VA // HEAD_DIM + g)),
        ],
        out_specs=pl.BlockSpec((None, tq, gw), lambda bi, g, qi: (bi, qi, g)),
        compiler_params=_cparams(("parallel", "parallel", "arbitrary")),
        name="attn_a",
    )(proj3, proj3, proj3)


HB_HEADS = 4
HB_W = HB_HEADS * HEAD_DIM
SUB = 128
WIN = SUB + 2 * HALF_WIN


def _attn_b_kernel(*refs, first, last, seq_len):
    it = iter(refs)
    q_ref, kl_ref, km_ref, kr_ref, vl_ref, vm_ref, vr_ref, bb_ref = (next(it) for _ in range(8))
    if not first:
        op_ref, lp_ref = next(it), next(it)
    o_ref = next(it)
    if not last:
        l_ref = next(it)
    kbuf, vbuf = next(it), next(it)

    tq = q_ref.shape[0]
    i = pl.program_id(3)
    kbuf[0:HALF_WIN, :] = kl_ref[...]
    kbuf[HALF_WIN:HALF_WIN + tq, :] = km_ref[...]
    kbuf[HALF_WIN + tq:, :] = kr_ref[...]
    vbuf[0:HALF_WIN, :] = vl_ref[...]
    vbuf[HALF_WIN:HALF_WIN + tq, :] = vm_ref[...]
    vbuf[HALF_WIN + tq:, :] = vr_ref[...]

    col = lax.broadcasted_iota(jnp.int32, (1, WIN), 1)
    for sub in range(tq // SUB):
        r0 = sub * SUB
        kidx = i * tq + (r0 - HALF_WIN) + col
        colmask = jnp.where((kidx >= 0) & (kidx < seq_len), 0.0, NEG_INF).astype(F32)
        for h in range(HB_HEADS):
            c0 = h * HEAD_DIM
            q = (q_ref[r0:r0 + SUB, c0:c0 + HEAD_DIM].astype(F32) * SCALE).astype(BF16)
            k = kbuf[r0:r0 + WIN, c0:c0 + HEAD_DIM]
            v = vbuf[r0:r0 + WIN, c0:c0 + HEAD_DIM]
            s = lax.dot_general(q, k, (((1,), (1,)), ((), ())), preferred_element_type=F32)
            s = s + bb_ref[h] + colmask
            m = jnp.max(s, axis=-1, keepdims=True)
            p = jnp.exp(s - m)
            l = jnp.sum(p, axis=-1, keepdims=True)
            o = jnp.dot(p.astype(BF16), v, preferred_element_type=F32) / l
            lse = m + jnp.log(l)
            if not first:
                lse_prev = lp_ref[r0:r0 + SUB, c0:c0 + HEAD_DIM]
                o_prev = op_ref[r0:r0 + SUB, c0:c0 + HEAD_DIM]
                m2 = jnp.maximum(lse_prev, lse)
                w0 = jnp.exp(lse_prev - m2)
                w1 = jnp.exp(lse - m2)
                den = w0 + w1
                o = (w0 * o_prev + w1 * o) / den
                lse = m2 + jnp.log(den)
            o_ref[r0:r0 + SUB, c0:c0 + HEAD_DIM] = o.astype(o_ref.dtype)
            if not last:
                l_ref[r0:r0 + SUB, c0:c0 + HEAD_DIM] = jnp.broadcast_to(lse, (SUB, HEAD_DIM))


def _band_bias(d):
    a = np.arange(SUB)[:, None]
    c = np.arange(WIN)[None, :]
    off = c - HALF_WIN - a
    slopes = (2.0 ** -np.arange(1, B_HEADS + 1)).astype(np.float32)
    bias = -slopes[:, None, None] * (np.abs(off) * d).astype(np.float32)[None]
    return np.where((np.abs(off) <= HALF_WIN)[None], bias, np.float32(NEG_INF)).astype(np.float32)


def _attn_b_pattern(proj3, prev, d, *, first, last):
    b, t, _ = proj3.shape
    seq_len = t // d
    tq = min(512, seq_len)
    nblk = seq_len // tq
    halo_per_blk = tq // HALF_WIN
    n_halo = seq_len // HALF_WIN
    cb = D_IN // HB_W
    pview = proj3.reshape(b, seq_len, d * D_IN)
    bb = jnp.asarray(_band_bias(d))

    def main(col0):
        return pl.BlockSpec((None, tq, HB_W),
                            lambda bi, r, hb, i: (bi, i, r * cb + col0 // HB_W + hb))

    def left(col0):
        return pl.BlockSpec((None, HALF_WIN, HB_W),
                            lambda bi, r, hb, i: (bi, jnp.maximum(i * halo_per_blk - 1, 0),
                                                  r * cb + col0 // HB_W + hb))

    def right(col0):
        return pl.BlockSpec((None, HALF_WIN, HB_W),
                            lambda bi, r, hb, i: (bi, jnp.minimum((i + 1) * halo_per_blk, n_halo - 1),
                                                  r * cb + col0 // HB_W + hb))

    state = pl.BlockSpec((None, tq, HB_W), lambda bi, r, hb, i: (bi, i, r * (D_B // HB_W) + hb))
    in_specs = [main(COL_QB), left(COL_KB), main(COL_KB), right(COL_KB),
                left(COL_VB), main(COL_VB), right(COL_VB),
                pl.BlockSpec((HB_HEADS, SUB, WIN), lambda bi, r, hb, i: (hb, 0, 0))]
    args = [pview] * 7 + [bb]
    if not first:
        o_prev, l_prev = prev
        in_specs += [state, state]
        args += [o_prev.reshape(b, seq_len, d * D_B), l_prev.reshape(b, seq_len, d * D_B)]
    if last:
        out_shape = jax.ShapeDtypeStruct((b, seq_len, d * D_B), BF16)
        out_specs = state
    else:
        out_shape = (jax.ShapeDtypeStruct((b, seq_len, d * D_B), F32),) * 2
        out_specs = (state, state)
    out = pl.pallas_call(
        functools.partial(_attn_b_kernel, first=first, last=last, seq_len=seq_len),
        out_shape=out_shape,
        grid=(b, d, B_HEADS // HB_HEADS, nblk),
        in_specs=in_specs,
        out_specs=out_specs,
        scratch_shapes=[pltpu.VMEM((tq + 2 * HALF_WIN, HB_W), BF16)] * 2,
        compiler_params=_cparams(("parallel", "parallel", "parallel", "arbitrary")),
        name=f"attn_b_d{d}",
    )(*args)
    if last:
        return out.reshape(b, t, D_B)
    return tuple(o.reshape(b, t, D_B) for o in out)


def _attn_b(proj3):
    state = None
    n = len(DILATED_PATTERNS)
    for idx, (w, d) in enumerate(DILATED_PATTERNS):
        assert w // (2 * d) == HALF_WIN
        state = _attn_b_pattern(proj3, state, d, first=idx == 0, last=idx == n - 1)
    return state


def _gated(y_ref, g_refs, gain_ref):
    y = y_ref[...].astype(F32)
    ms = jnp.mean(y * y, axis=-1, keepdims=True)
    yn = y * lax.rsqrt(ms + EPS) * gain_ref[...]
    g = jnp.concatenate([r[...] for r in g_refs], axis=-1).astype(F32)
    return (yn * (g * jax.nn.sigmoid(g))).astype(BF16)


def _outproj_kernel(x_ref, ya_ref, yb_ref, ga0_ref, ga1_ref, gb0_ref, gb1_ref, na_ref, nb_ref,
                    w_ref, fn_ref, o_ref, *, final):
    za = _gated(ya_ref, (ga0_ref, ga1_ref), na_ref)
    zb = _gated(yb_ref, (gb0_ref, gb1_ref), nb_ref)
    out = x_ref[...]
    out = out + jnp.dot(za, w_ref[0:D_A, :], preferred_element_type=F32)
    out = out + jnp.dot(zb, w_ref[D_A:, :], preferred_element_type=F32)
    if final:
        ms = jnp.mean(out * out, axis=-1, keepdims=True)
        out = out * lax.rsqrt(ms + EPS) * fn_ref[...]
    o_ref[...] = out


def _outproj(x2, ya2, yb2, proj2, na, nb, w, fn, *, final, tm=512):
    m = x2.shape[0]
    half = D_A // 2

    def gate(col0, k):
        return pl.BlockSpec((tm, half), lambda i: (i, col0 // half + k))

    row = lambda width: pl.BlockSpec((tm, width), lambda i: (i, 0))
    const = lambda shape: pl.BlockSpec(shape, lambda i: (0, 0))
    return pl.pallas_call(
        functools.partial(_outproj_kernel, final=final),
        out_shape=jax.ShapeDtypeStruct((m, D_MODEL), F32),
        grid=(m // tm,),
        in_specs=[row(D_MODEL), row(D_A), row(D_B),
                  gate(COL_GA, 0), gate(COL_GA, 1), gate(COL_GB, 0), gate(COL_GB, 1),
                  const((1, D_A)), const((1, D_B)), const((D_A + D_B, D_MODEL)), const((1, D_MODEL))],
        out_specs=row(D_MODEL),
        compiler_params=_cparams(("parallel",)),
        name="outproj",
    )(x2, ya2, yb2, proj2, proj2, proj2, proj2, na, nb, w, fn)


def _rope_tables(t):
    pos = jnp.arange(t, dtype=jnp.int32)
    row = (pos // GRID_W).astype(F32)
    col = (pos % GRID_W).astype(F32)
    inv_freq = ROPE_THETA ** (-jnp.arange(0, AXIS_DIM, 2, dtype=F32) / AXIS_DIM)
    ang_r = row[:, None] * inv_freq[None, :]
    ang_c = col[:, None] * inv_freq[None, :]
    cos = jnp.concatenate([jnp.cos(ang_r)] * 2 + [jnp.cos(ang_c)] * 2, axis=-1)
    sin = jnp.concatenate([-jnp.sin(ang_r), jnp.sin(ang_r), -jnp.sin(ang_c), jnp.sin(ang_c)], axis=-1)
    return cos, sin


def kernel(x, norm_w, w_in, q_norm_a, k_norm_a, out_norm_a, out_norm_b, w_out, final_norm):
    b, t, d_model = x.shape
    depth = w_in.shape[0]
    assert d_model == D_MODEL and w_in.shape[1:] == (D_MODEL, D_IN)
    assert t % (DILATED_PATTERNS[-1][1] * SUB) == 0
    cos, sin = _rope_tables(t)
    w_in_bf = w_in.astype(BF16)
    w_out_bf = w_out.astype(BF16)
    x2 = x.reshape(b * t, D_MODEL)
    for l in range(depth):
        proj2 = _inproj(x2, norm_w[l][None], w_in_bf[l], (q_norm_a[l] * SCALE)[None], k_norm_a[l][None],
                        cos, sin, seq=t)
        proj3 = proj2.reshape(b, t, D_IN)
        ya = _attn_a(proj3)
        yb = _attn_b(proj3)
        x2 = _outproj(x2, ya.reshape(b * t, D_A), yb.reshape(b * t, D_B), proj2,
                      out_norm_a[l][None], out_norm_b[l][None], w_out_bf[l], final_norm[None],
                      final=l == depth - 1)
    return x2.reshape(b, t, D_MODEL)
```

```python
import functools

import numpy as np
import jax
import jax.numpy as jnp
from jax import lax
from jax.experimental import pallas as pl
from jax.experimental.pallas import tpu as pltpu

D_MODEL = 2048
HEAD_DIM = 128
A_HEADS = 8
A_KV_HEADS = 2
A_GROUP = A_HEADS // A_KV_HEADS
B_HEADS = 8
D_A = A_HEADS * HEAD_DIM
D_B = B_HEADS * HEAD_DIM
KV_DIM = A_KV_HEADS * HEAD_DIM
D_IN = 2 * D_A + 2 * KV_DIM + 4 * D_B
GRID_W = 64
AXIS_DIM = HEAD_DIM // 2
ROPE_THETA = 10000.0
DILATED_PATTERNS = ((128, 1), (512, 4), (2048, 16))
HALF_WIN = 64
SCALE = HEAD_DIM ** -0.5
LOG2E = float(np.log2(np.e))
EPS = 1e-6
NEG_INF = -1e30

COL_QA, COL_KA, COL_VA, COL_GA = 0, D_A, D_A + KV_DIM, D_A + 2 * KV_DIM
COL_QB = COL_GA + D_A
COL_KB, COL_VB, COL_GB = COL_QB + D_B, COL_QB + 2 * D_B, COL_QB + 3 * D_B

LANES = 128
VMEM_LIMIT = 56 * 1024 * 1024

BF16 = jnp.bfloat16
F32 = jnp.float32


def _cparams(sem):
    return pltpu.CompilerParams(dimension_semantics=sem, vmem_limit_bytes=VMEM_LIMIT)


def _rope_group(a, gain, cos, sin):
    ms = jnp.mean(a * a, axis=-1, keepdims=True)
    y = a * lax.rsqrt(ms + EPS) * gain
    lane = lax.broadcasted_iota(jnp.int32, y.shape, 1)
    partner = jnp.where((lane % AXIS_DIM) < AXIS_DIM // 2,
                        pltpu.roll(y, LANES - AXIS_DIM // 2, 1), pltpu.roll(y, AXIS_DIM // 2, 1))
    return y * cos + partner * sin


def _inproj_kernel(x_ref, nw_ref, w_ref, qg_ref, kg_ref, cos_ref, sin_ref, o_ref, h_ref, *, tn):
    j = pl.program_id(1)

    @pl.when(j == 0)
    def _():
        x = x_ref[...]
        ms = jnp.mean(x * x, axis=-1, keepdims=True)
        h_ref[...] = (x * lax.rsqrt(ms + EPS) * nw_ref[...]).astype(BF16)

    acc = jnp.dot(h_ref[...], w_ref[...], preferred_element_type=F32)

    groups_per_tile = tn // HEAD_DIM
    n_q, n_k = D_A // HEAD_DIM, KV_DIM // HEAD_DIM
    n_special = -(-(n_q + n_k) // groups_per_tile)

    def emit(tile):
        for c in range(groups_per_tile):
            g = tile * groups_per_tile + c
            a = acc[:, c * HEAD_DIM:(c + 1) * HEAD_DIM]
            if g < n_q:
                a = _rope_group(a, qg_ref[...], cos_ref[...], sin_ref[...])
            elif g < n_q + n_k:
                a = _rope_group(a, kg_ref[...], cos_ref[...], sin_ref[...])
            o_ref[:, c * HEAD_DIM:(c + 1) * HEAD_DIM] = a.astype(o_ref.dtype)

    for tile in range(n_special):
        pl.when(j == tile)(functools.partial(emit, tile))

    @pl.when(j >= n_special)
    def _():
        o_ref[...] = acc.astype(o_ref.dtype)


def _inproj(x2, nw, w, qg, kg, cos, sin, *, seq, tm=1024, tn=512):
    m = x2.shape[0]
    t_blocks = seq // tm
    return pl.pallas_call(
        functools.partial(_inproj_kernel, tn=tn),
        out_shape=jax.ShapeDtypeStruct((m, D_IN), BF16),
        grid=(m // tm, D_IN // tn),
        in_specs=[
            pl.BlockSpec((tm, D_MODEL), lambda i, j: (i, 0)),
            pl.BlockSpec((1, D_MODEL), lambda i, j: (0, 0)),
            pl.BlockSpec((D_MODEL, tn), lambda i, j: (0, j)),
            pl.BlockSpec((1, HEAD_DIM), lambda i, j: (0, 0)),
            pl.BlockSpec((1, HEAD_DIM), lambda i, j: (0, 0)),
            pl.BlockSpec((tm, HEAD_DIM), lambda i, j: (i % t_blocks, 0)),
            pl.BlockSpec((tm, HEAD_DIM), lambda i, j: (i % t_blocks, 0)),
        ],
        out_specs=pl.BlockSpec((tm, tn), lambda i, j: (i, j)),
        scratch_shapes=[pltpu.VMEM((tm, D_MODEL), BF16)],
        compiler_params=_cparams(("parallel", "arbitrary")),
        name="inproj",
    )(x2, nw, w, qg, kg, cos, sin)


ONES_ROWS = 16


def _attn_a_kernel(q_ref, k_ref, v_ref, o_ref, vt_ref, s0_ref, s1_ref, acc_ref, *, tk):
    tq = q_ref.shape[0]
    n_kv = k_ref.shape[0] // tk

    @pl.when(pl.program_id(2) == 0)
    def _():
        for kk in range(n_kv):
            vt_ref[kk, 0:HEAD_DIM, :] = v_ref[kk * tk:(kk + 1) * tk, :].astype(F32).T.astype(BF16)
            vt_ref[kk, HEAD_DIM:, :] = jnp.ones((ONES_ROWS, tk), BF16)

    sbuf = (s0_ref, s1_ref)

    def scores(kk, dst):
        start = pl.multiple_of(kk * tk, tk)
        k = k_ref[pl.ds(start, tk), :]
        for hh in range(A_GROUP):
            q = q_ref[:, hh * HEAD_DIM:(hh + 1) * HEAD_DIM]
            dst[hh] = lax.dot_general(k, q, (((1,), (1,)), ((), ())), preferred_element_type=F32)

    def softmax_pv(kk, src, ml):
        vt = vt_ref[kk]
        new = []
        for hh in range(A_GROUP):
            m, l = ml[hh]
            s = src[hh]
            m_new = jnp.maximum(m, jnp.max(s, axis=0, keepdims=True))
            alpha = jnp.exp2(m - m_new)
            p = jnp.exp2(s - m_new).astype(BF16)
            pv = jnp.dot(vt, p, preferred_element_type=F32)
            acc_ref[hh] = alpha * acc_ref[hh] + pv[0:HEAD_DIM]
            new.append((m_new, alpha * l + pv[HEAD_DIM:HEAD_DIM + 1]))
        return tuple(new)

    def pair(jj, ml):
        kk = 2 * jj
        scores(kk + 1, sbuf[1])
        ml = softmax_pv(kk, sbuf[0], ml)
        scores(kk + 2, sbuf[0])
        return softmax_pv(kk + 1, sbuf[1], ml)

    acc_ref[...] = jnp.zeros_like(acc_ref)
    scores(0, sbuf[0])
    ml = tuple((jnp.full((1, tq), -jnp.inf, F32), jnp.zeros((1, tq), F32)) for _ in range(A_GROUP))
    ml = lax.fori_loop(0, n_kv // 2 - 1, pair, ml)
    scores(n_kv - 1, sbuf[1])
    ml = softmax_pv(n_kv - 2, sbuf[0], ml)
    ml = softmax_pv(n_kv - 1, sbuf[1], ml)
    for hh in range(A_GROUP):
        o_ref[:, hh * HEAD_DIM:(hh + 1) * HEAD_DIM] = (acc_ref[hh] / ml[hh][1]).T.astype(o_ref.dtype)


def _attn_a(proj3, *, tq=256, tk=512):
    b, t, _ = proj3.shape
    gw = A_GROUP * HEAD_DIM
    return pl.pallas_call(
        functools.partial(_attn_a_kernel, tk=tk),
        out_shape=jax.ShapeDtypeStruct((b, t, D_A), BF16),
        grid=(b, A_KV_HEADS, t // tq),
        in_specs=[
            pl.BlockSpec((None, tq, gw), lambda bi, g, qi: (bi, qi, COL_QA // gw + g)),
            pl.BlockSpec((None, t, HEAD_DIM), lambda bi, g, qi: (bi, 0, COL_KA // HEAD_DIM + g)),
            pl.BlockSpec((None, t, HEAD_DIM), lambda bi, g, qi: (bi, 0, COL_VA // HEAD_DIM + g)),
        ],
        out_specs=pl.BlockSpec((None, tq, gw), lambda bi, g, qi: (bi, qi, g)),
        scratch_shapes=[pltpu.VMEM((t // tk, HEAD_DIM + ONES_ROWS, tk), BF16),
                        pltpu.VMEM((A_GROUP, tk, tq), F32), pltpu.VMEM((A_GROUP, tk, tq), F32),
                        pltpu.VMEM((A_GROUP, HEAD_DIM, tq), F32)],
        compiler_params=_cparams(("parallel", "parallel", "arbitrary")),
        name="attn_a",
    )(proj3, proj3, proj3)


HB_HEADS = 4
HB_W = HB_HEADS * HEAD_DIM
SUB = 128
WIN = SUB + 2 * HALF_WIN


def _attn_b_kernel(*refs, first, last, seq_len):
    it = iter(refs)
    q_ref, kl_ref, km_ref, kr_ref, vl_ref, vm_ref, vr_ref, bb_ref = (next(it) for _ in range(8))
    if not first:
        op_ref, lp_ref = next(it), next(it)
    o_ref = next(it)
    if not last:
        l_ref = next(it)
    kbuf, vbuf = next(it), next(it)

    tq = q_ref.shape[0]
    i = pl.program_id(3)
    kbuf[0:HALF_WIN, :] = kl_ref[...]
    kbuf[HALF_WIN:HALF_WIN + tq, :] = km_ref[...]
    kbuf[HALF_WIN + tq:, :] = kr_ref[...]
    vbuf[0:HALF_WIN, :] = vl_ref[...]
    vbuf[HALF_WIN:HALF_WIN + tq, :] = vm_ref[...]
    vbuf[HALF_WIN + tq:, :] = vr_ref[...]

    col = lax.broadcasted_iota(jnp.int32, (1, WIN), 1)
    for sub in range(tq // SUB):
        r0 = sub * SUB
        kidx = i * tq + (r0 - HALF_WIN) + col
        colmask = jnp.where((kidx >= 0) & (kidx < seq_len), 0.0, NEG_INF).astype(F32)
        for h in range(HB_HEADS):
            c0 = h * HEAD_DIM
            q = (q_ref[r0:r0 + SUB, c0:c0 + HEAD_DIM].astype(F32) * SCALE).astype(BF16)
            k = kbuf[r0:r0 + WIN, c0:c0 + HEAD_DIM]
            v = vbuf[r0:r0 + WIN, c0:c0 + HEAD_DIM]
            s = lax.dot_general(q, k, (((1,), (1,)), ((), ())), preferred_element_type=F32)
            s = s + bb_ref[h] + colmask
            m = jnp.max(s, axis=-1, keepdims=True)
            p = jnp.exp(s - m)
            l = jnp.sum(p, axis=-1, keepdims=True)
            o = jnp.dot(p.astype(BF16), v, preferred_element_type=F32) / l
            lse = m + jnp.log(l)
            if not first:
                lse_prev = lp_ref[r0:r0 + SUB, c0:c0 + HEAD_DIM]
                o_prev = op_ref[r0:r0 + SUB, c0:c0 + HEAD_DIM]
                m2 = jnp.maximum(lse_prev, lse)
                w0 = jnp.exp(lse_prev - m2)
                w1 = jnp.exp(lse - m2)
                den = w0 + w1
                o = (w0 * o_prev + w1 * o) / den
                lse = m2 + jnp.log(den)
            o_ref[r0:r0 + SUB, c0:c0 + HEAD_DIM] = o.astype(o_ref.dtype)
            if not last:
                l_ref[r0:r0 + SUB, c0:c0 + HEAD_DIM] = jnp.broadcast_to(lse, (SUB, HEAD_DIM))


def _band_bias(d):
    a = np.arange(SUB)[:, None]
    c = np.arange(WIN)[None, :]
    off = c - HALF_WIN - a
    slopes = (2.0 ** -np.arange(1, B_HEADS + 1)).astype(np.float32)
    bias = -slopes[:, None, None] * (np.abs(off) * d).astype(np.float32)[None]
    return np.where((np.abs(off) <= HALF_WIN)[None], bias, np.float32(NEG_INF)).astype(np.float32)


def _attn_b_pattern(proj3, prev, d, *, first, last):
    b, t, _ = proj3.shape
    seq_len = t // d
    tq = min(512, seq_len)
    nblk = seq_len // tq
    halo_per_blk = tq // HALF_WIN
    n_halo = seq_len // HALF_WIN
    cb = D_IN // HB_W
    pview = proj3.reshape(b, seq_len, d * D_IN)
    bb = jnp.asarray(_band_bias(d))

    def main(col0):
        return pl.BlockSpec((None, tq, HB_W),
                            lambda bi, r, hb, i: (bi, i, r * cb + col0 // HB_W + hb))

    def left(col0):
        return pl.BlockSpec((None, HALF_WIN, HB_W),
                            lambda bi, r, hb, i: (bi, jnp.maximum(i * halo_per_blk - 1, 0),
                                                  r * cb + col0 // HB_W + hb))

    def right(col0):
        return pl.BlockSpec((None, HALF_WIN, HB_W),
                            lambda bi, r, hb, i: (bi, jnp.minimum((i + 1) * halo_per_blk, n_halo - 1),
                                                  r * cb + col0 // HB_W + hb))

    state = pl.BlockSpec((None, tq, HB_W), lambda bi, r, hb, i: (bi, i, r * (D_B // HB_W) + hb))
    in_specs = [main(COL_QB), left(COL_KB), main(COL_KB), right(COL_KB),
                left(COL_VB), main(COL_VB), right(COL_VB),
                pl.BlockSpec((HB_HEADS, SUB, WIN), lambda bi, r, hb, i: (hb, 0, 0))]
    args = [pview] * 7 + [bb]
    if not first:
        o_prev, l_prev = prev
        in_specs += [state, state]
        args += [o_prev.reshape(b, seq_len, d * D_B), l_prev.reshape(b, seq_len, d * D_B)]
    if last:
        out_shape = jax.ShapeDtypeStruct((b, seq_len, d * D_B), BF16)
        out_specs = state
    else:
        out_shape = (jax.ShapeDtypeStruct((b, seq_len, d * D_B), F32),) * 2
        out_specs = (state, state)
    out = pl.pallas_call(
        functools.partial(_attn_b_kernel, first=first, last=last, seq_len=seq_len),
        out_shape=out_shape,
        grid=(b, d, B_HEADS // HB_HEADS, nblk),
        in_specs=in_specs,
        out_specs=out_specs,
        scratch_shapes=[pltpu.VMEM((tq + 2 * HALF_WIN, HB_W), BF16)] * 2,
        compiler_params=_cparams(("parallel", "parallel", "parallel", "arbitrary")),
        name=f"attn_b_d{d}",
    )(*args)
    if last:
        return out.reshape(b, t, D_B)
    return tuple(o.reshape(b, t, D_B) for o in out)


def _attn_b(proj3):
    state = None
    n = len(DILATED_PATTERNS)
    for idx, (w, d) in enumerate(DILATED_PATTERNS):
        assert w // (2 * d) == HALF_WIN
        state = _attn_b_pattern(proj3, state, d, first=idx == 0, last=idx == n - 1)
    return state


def _gated(y_ref, g_refs, gain_ref):
    y = y_ref[...].astype(F32)
    ms = jnp.mean(y * y, axis=-1, keepdims=True)
    yn = y * lax.rsqrt(ms + EPS) * gain_ref[...]
    g = jnp.concatenate([r[...] for r in g_refs], axis=-1).astype(F32)
    return (yn * (g * jax.nn.sigmoid(g))).astype(BF16)


def _outproj_kernel(x_ref, ya_ref, yb_ref, ga0_ref, ga1_ref, gb0_ref, gb1_ref, na_ref, nb_ref,
                    w_ref, fn_ref, o_ref, *, final):
    za = _gated(ya_ref, (ga0_ref, ga1_ref), na_ref)
    zb = _gated(yb_ref, (gb0_ref, gb1_ref), nb_ref)
    out = x_ref[...]
    out = out + jnp.dot(za, w_ref[0:D_A, :], preferred_element_type=F32)
    out = out + jnp.dot(zb, w_ref[D_A:, :], preferred_element_type=F32)
    if final:
        ms = jnp.mean(out * out, axis=-1, keepdims=True)
        out = out * lax.rsqrt(ms + EPS) * fn_ref[...]
    o_ref[...] = out


def _outproj(x2, ya2, yb2, proj2, na, nb, w, fn, *, final, tm=512):
    m = x2.shape[0]
    half = D_A // 2

    def gate(col0, k):
        return pl.BlockSpec((tm, half), lambda i: (i, col0 // half + k))

    row = lambda width: pl.BlockSpec((tm, width), lambda i: (i, 0))
    const = lambda shape: pl.BlockSpec(shape, lambda i: (0, 0))
    return pl.pallas_call(
        functools.partial(_outproj_kernel, final=final),
        out_shape=jax.ShapeDtypeStruct((m, D_MODEL), F32),
        grid=(m // tm,),
        in_specs=[row(D_MODEL), row(D_A), row(D_B),
                  gate(COL_GA, 0), gate(COL_GA, 1), gate(COL_GB, 0), gate(COL_GB, 1),
                  const((1, D_A)), const((1, D_B)), const((D_A + D_B, D_MODEL)), const((1, D_MODEL))],
        out_specs=row(D_MODEL),
        compiler_params=_cparams(("parallel",)),
        name="outproj",
    )(x2, ya2, yb2, proj2, proj2, proj2, proj2, na, nb, w, fn)


def _rope_tables(t):
    pos = jnp.arange(t, dtype=jnp.int32)
    row = (pos // GRID_W).astype(F32)
    col = (pos % GRID_W).astype(F32)
    inv_freq = ROPE_THETA ** (-jnp.arange(0, AXIS_DIM, 2, dtype=F32) / AXIS_DIM)
    ang_r = row[:, None] * inv_freq[None, :]
    ang_c = col[:, None] * inv_freq[None, :]
    cos = jnp.concatenate([jnp.cos(ang_r)] * 2 + [jnp.cos(ang_c)] * 2, axis=-1)
    sin = jnp.concatenate([-jnp.sin(ang_r), jnp.sin(ang_r), -jnp.sin(ang_c), jnp.sin(ang_c)], axis=-1)
    return cos, sin


def kernel(x, norm_w, w_in, q_norm_a, k_norm_a, out_norm_a, out_norm_b, w_out, final_norm):
    b, t, d_model = x.shape
    depth = w_in.shape[0]
    assert d_model == D_MODEL and w_in.shape[1:] == (D_MODEL, D_IN)
    assert t % (DILATED_PATTERNS[-1][1] * SUB) == 0
    cos, sin = _rope_tables(t)
    w_in_bf = w_in.astype(BF16)
    w_out_bf = w_out.astype(BF16)
    x2 = x.reshape(b * t, D_MODEL)
    for l in range(depth):
        proj2 = _inproj(x2, norm_w[l][None], w_in_bf[l], (q_norm_a[l] * (SCALE * LOG2E))[None], k_norm_a[l][None],
                        cos, sin, seq=t)
        proj3 = proj2.reshape(b, t, D_IN)
        ya = _attn_a(proj3)
        yb = _attn_b(proj3)
        x2 = _outproj(x2, ya.reshape(b * t, D_A), yb.reshape(b * t, D_B), proj2,
                      out_norm_a[l][None], out_norm_b[l][None], w_out_bf[l], final_norm[None],
                      final=l == depth - 1)
    return x2.reshape(b, t, D_MODEL)
```

```python
import functools

import numpy as np
import jax
import jax.numpy as jnp
from jax import lax
from jax.experimental import pallas as pl
from jax.experimental.pallas import tpu as pltpu

D_MODEL = 2048
HEAD_DIM = 128
A_HEADS = 8
A_KV_HEADS = 2
A_GROUP = A_HEADS // A_KV_HEADS
B_HEADS = 8
D_A = A_HEADS * HEAD_DIM
D_B = B_HEADS * HEAD_DIM
KV_DIM = A_KV_HEADS * HEAD_DIM
D_IN = 2 * D_A + 2 * KV_DIM + 4 * D_B
GRID_W = 64
AXIS_DIM = HEAD_DIM // 2
ROPE_THETA = 10000.0
DILATED_PATTERNS = ((128, 1), (512, 4), (2048, 16))
HALF_WIN = 64
SCALE = HEAD_DIM ** -0.5
LOG2E = float(np.log2(np.e))
EPS = 1e-6
NEG_INF = -1e30

COL_QA, COL_KA, COL_VA, COL_GA = 0, D_A, D_A + KV_DIM, D_A + 2 * KV_DIM
COL_QB = COL_GA + D_A
COL_KB, COL_VB, COL_GB = COL_QB + D_B, COL_QB + 2 * D_B, COL_QB + 3 * D_B

LANES = 128
VMEM_LIMIT = 56 * 1024 * 1024

BF16 = jnp.bfloat16
F32 = jnp.float32


def _cparams(sem):
    return pltpu.CompilerParams(dimension_semantics=sem, vmem_limit_bytes=VMEM_LIMIT)


def _rope_group(a, gain, cos, sin):
    ms = jnp.mean(a * a, axis=-1, keepdims=True)
    y = a * lax.rsqrt(ms + EPS) * gain
    lane = lax.broadcasted_iota(jnp.int32, y.shape, 1)
    partner = jnp.where((lane % AXIS_DIM) < AXIS_DIM // 2,
                        pltpu.roll(y, LANES - AXIS_DIM // 2, 1), pltpu.roll(y, AXIS_DIM // 2, 1))
    return y * cos + partner * sin


STRIDES = tuple(d for _, d in DILATED_PATTERNS if d > 1)
QKVB_W = COL_GB - COL_QB


def _inproj_kernel(x_ref, nw_ref, w_ref, qg_ref, kg_ref, cos_ref, sin_ref, o_ref, *rest, tn):
    strided_refs, (h_ref, slab_ref) = rest[:len(STRIDES)], rest[len(STRIDES):]
    j = pl.program_id(1)

    @pl.when(j == 0)
    def _():
        x = x_ref[...]
        ms = jnp.mean(x * x, axis=-1, keepdims=True)
        h_ref[...] = (x * lax.rsqrt(ms + EPS) * nw_ref[...]).astype(BF16)

    acc = jnp.dot(h_ref[...], w_ref[...], preferred_element_type=F32)

    groups_per_tile = tn // HEAD_DIM
    n_q, n_k = D_A // HEAD_DIM, KV_DIM // HEAD_DIM
    n_special = -(-(n_q + n_k) // groups_per_tile)

    def emit(tile):
        for c in range(groups_per_tile):
            g = tile * groups_per_tile + c
            a = acc[:, c * HEAD_DIM:(c + 1) * HEAD_DIM]
            if g < n_q:
                a = _rope_group(a, qg_ref[...], cos_ref[...], sin_ref[...])
            elif g < n_q + n_k:
                a = _rope_group(a, kg_ref[...], cos_ref[...], sin_ref[...])
            o_ref[:, c * HEAD_DIM:(c + 1) * HEAD_DIM] = a.astype(o_ref.dtype)

    for tile in range(n_special):
        pl.when(j == tile)(functools.partial(emit, tile))

    @pl.when(j >= n_special)
    def _():
        o_ref[...] = acc.astype(o_ref.dtype)

    @pl.when((j >= COL_QB // tn) & (j < COL_GB // tn))
    def _():
        tm = acc.shape[0]
        for c in range(tn // LANES):
            slab_ref[c] = acc[:, c * LANES:(c + 1) * LANES]
        for d, ref in zip(STRIDES, strided_refs):
            for c in range(tn // LANES):
                for r in range(d):
                    ref[r, :, c * LANES:(c + 1) * LANES] = (
                        slab_ref[c, pl.ds(r, tm // d, stride=d), :].astype(ref.dtype))


def _inproj(x2, nw, w, qg, kg, cos, sin, *, seq, tm=1024, tn=512):
    m = x2.shape[0]
    t_blocks = seq // tm
    batch = m // seq
    assert COL_QB % tn == 0 and COL_GB % tn == 0
    first_b, n_b = COL_QB // tn, QKVB_W // tn

    def strided_spec(d):
        return pl.BlockSpec((None, d, tm // d, tn),
                            lambda i, j: (i // t_blocks, 0, i % t_blocks, jnp.clip(j - first_b, 0, n_b - 1)))

    return pl.pallas_call(
        functools.partial(_inproj_kernel, tn=tn),
        out_shape=(jax.ShapeDtypeStruct((m, D_IN), BF16),)
        + tuple(jax.ShapeDtypeStruct((batch, d, seq // d, QKVB_W), BF16) for d in STRIDES),
        grid=(m // tm, D_IN // tn),
        in_specs=[
            pl.BlockSpec((tm, D_MODEL), lambda i, j: (i, 0)),
            pl.BlockSpec((1, D_MODEL), lambda i, j: (0, 0)),
            pl.BlockSpec((D_MODEL, tn), lambda i, j: (0, j)),
            pl.BlockSpec((1, HEAD_DIM), lambda i, j: (0, 0)),
            pl.BlockSpec((1, HEAD_DIM), lambda i, j: (0, 0)),
            pl.BlockSpec((tm, HEAD_DIM), lambda i, j: (i % t_blocks, 0)),
            pl.BlockSpec((tm, HEAD_DIM), lambda i, j: (i % t_blocks, 0)),
        ],
        out_specs=(pl.BlockSpec((tm, tn), lambda i, j: (i, j)),) + tuple(strided_spec(d) for d in STRIDES),
        scratch_shapes=[pltpu.VMEM((tm, D_MODEL), BF16), pltpu.VMEM((tn // LANES, tm, LANES), F32)],
        compiler_params=_cparams(("arbitrary", "arbitrary")),
        name="inproj",
    )(x2, nw, w, qg, kg, cos, sin)


ONES_ROWS = 16


def _attn_a_kernel(q_ref, k_ref, v_ref, o_ref, vt_ref, s0_ref, s1_ref, acc_ref, *, tk):
    tq = q_ref.shape[0]
    n_kv = k_ref.shape[0] // tk

    @pl.when(pl.program_id(2) == 0)
    def _():
        for kk in range(n_kv):
            vt_ref[kk, 0:HEAD_DIM, :] = v_ref[kk * tk:(kk + 1) * tk, :].astype(F32).T.astype(BF16)
            vt_ref[kk, HEAD_DIM:, :] = jnp.ones((ONES_ROWS, tk), BF16)

    sbuf = (s0_ref, s1_ref)

    def scores(kk, dst):
        start = pl.multiple_of(kk * tk, tk)
        k = k_ref[pl.ds(start, tk), :]
        for hh in range(A_GROUP):
            q = q_ref[:, hh * HEAD_DIM:(hh + 1) * HEAD_DIM]
            dst[hh] = lax.dot_general(k, q, (((1,), (1,)), ((), ())), preferred_element_type=F32)

    def softmax_pv(kk, src, ml):
        vt = vt_ref[kk]
        new = []
        for hh in range(A_GROUP):
            m, l = ml[hh]
            s = src[hh]
            m_new = jnp.maximum(m, jnp.max(s, axis=0, keepdims=True))
            alpha = jnp.exp2(m - m_new)
            p = jnp.exp2(s - m_new).astype(BF16)
            pv = jnp.dot(vt, p, preferred_element_type=F32)
            acc_ref[hh] = alpha * acc_ref[hh] + pv[0:HEAD_DIM]
            new.append((m_new, alpha * l + pv[HEAD_DIM:HEAD_DIM + 1]))
        return tuple(new)

    def pair(jj, ml):
        kk = 2 * jj
        scores(kk + 1, sbuf[1])
        ml = softmax_pv(kk, sbuf[0], ml)
        scores(kk + 2, sbuf[0])
        return softmax_pv(kk + 1, sbuf[1], ml)

    acc_ref[...] = jnp.zeros_like(acc_ref)
    scores(0, sbuf[0])
    ml = tuple((jnp.full((1, tq), -jnp.inf, F32), jnp.zeros((1, tq), F32)) for _ in range(A_GROUP))
    ml = lax.fori_loop(0, n_kv // 2 - 1, pair, ml)
    scores(n_kv - 1, sbuf[1])
    ml = softmax_pv(n_kv - 2, sbuf[0], ml)
    ml = softmax_pv(n_kv - 1, sbuf[1], ml)
    for hh in range(A_GROUP):
        o_ref[:, hh * HEAD_DIM:(hh + 1) * HEAD_DIM] = (acc_ref[hh] / ml[hh][1]).T.astype(o_ref.dtype)


def _attn_a(proj3, *, tq=256, tk=512):
    b, t, _ = proj3.shape
    gw = A_GROUP * HEAD_DIM
    return pl.pallas_call(
        functools.partial(_attn_a_kernel, tk=tk),
        out_shape=jax.ShapeDtypeStruct((b, t, D_A), BF16),
        grid=(b, A_KV_HEADS, t // tq),
        in_specs=[
            pl.BlockSpec((None, tq, gw), lambda bi, g, qi: (bi, qi, COL_QA // gw + g)),
            pl.BlockSpec((None, t, HEAD_DIM), lambda bi, g, qi: (bi, 0, COL_KA // HEAD_DIM + g)),
            pl.BlockSpec((None, t, HEAD_DIM), lambda bi, g, qi: (bi, 0, COL_VA // HEAD_DIM + g)),
        ],
        out_specs=pl.BlockSpec((None, tq, gw), lambda bi, g, qi: (bi, qi, g)),
        scratch_shapes=[pltpu.VMEM((t // tk, HEAD_DIM + ONES_ROWS, tk), BF16),
                        pltpu.VMEM((A_GROUP, tk, tq), F32), pltpu.VMEM((A_GROUP, tk, tq), F32),
                        pltpu.VMEM((A_GROUP, HEAD_DIM, tq), F32)],
        compiler_params=_cparams(("parallel", "parallel", "arbitrary")),
        name="attn_a",
    )(proj3, proj3, proj3)


HB_HEADS = 4
HB_W = HB_HEADS * HEAD_DIM
SUB = 128
WIN = SUB + 2 * HALF_WIN
LSE_REP = LANES // HB_HEADS


def _attn_b_kernel(*refs, merged, seq_len):
    it = iter(refs)
    q_ref, kl_ref, km_ref, kr_ref, vl_ref, vm_ref, vr_ref, bb_ref = (next(it) for _ in range(8))
    parts = [(next(it), next(it)) for _ in merged]
    o_ref = next(it)
    l_ref = None if merged else next(it)
    kbuf, vbuf = next(it), next(it)
    flat = [(next(it), next(it)) for _ in merged]

    tq = q_ref.shape[0]
    i = pl.program_id(3)
    kbuf[0:HALF_WIN, :] = kl_ref[...]
    kbuf[HALF_WIN:HALF_WIN + tq, :] = km_ref[...]
    kbuf[HALF_WIN + tq:, :] = kr_ref[...]
    vbuf[0:HALF_WIN, :] = vl_ref[...]
    vbuf[HALF_WIN:HALF_WIN + tq, :] = vm_ref[...]
    vbuf[HALF_WIN + tq:, :] = vr_ref[...]

    for d, (po_ref, pl_ref), (fo_ref, fl_ref) in zip(merged, parts, flat):
        for r in range(d):
            fl_ref[pl.ds(r, tq // d, stride=d), :] = pl_ref[r]
            for h in range(HB_HEADS):
                fo_ref[h, pl.ds(r, tq // d, stride=d), :] = po_ref[r, :, h * HEAD_DIM:(h + 1) * HEAD_DIM]

    col = lax.broadcasted_iota(jnp.int32, (1, WIN), 1)
    lane_head = lax.broadcasted_iota(jnp.int32, (SUB, LANES), 1) // LSE_REP
    for sub in range(tq // SUB):
        r0 = sub * SUB
        kidx = i * tq + (r0 - HALF_WIN) + col
        colmask = jnp.where((kidx >= 0) & (kidx < seq_len), 0.0, NEG_INF).astype(F32)
        stats = jnp.zeros((SUB, LANES), F32)
        for h in range(HB_HEADS):
            c0 = h * HEAD_DIM
            q = (q_ref[r0:r0 + SUB, c0:c0 + HEAD_DIM].astype(F32) * SCALE).astype(BF16)
            k = kbuf[r0:r0 + WIN, c0:c0 + HEAD_DIM]
            v = vbuf[r0:r0 + WIN, c0:c0 + HEAD_DIM]
            s = lax.dot_general(q, k, (((1,), (1,)), ((), ())), preferred_element_type=F32)
            s = s + bb_ref[h] + colmask
            m = jnp.max(s, axis=-1, keepdims=True)
            p = jnp.exp(s - m)
            l = jnp.sum(p, axis=-1, keepdims=True)
            o = jnp.dot(p.astype(BF16), v, preferred_element_type=F32) / l
            lse = m + jnp.log(l)
            if merged:
                lses = [lse] + [fl_ref[r0:r0 + SUB, h * LSE_REP:h * LSE_REP + 1] for _, fl_ref in flat]
                outs = [o] + [fo_ref[h, r0:r0 + SUB, :] for fo_ref, _ in flat]
                top = functools.reduce(jnp.maximum, lses)
                ws = [jnp.exp(x - top) for x in lses]
                o = sum(w * x for w, x in zip(ws, outs)) / sum(ws)
            else:
                stats = jnp.where(lane_head == h, lse, stats)
            o_ref[r0:r0 + SUB, c0:c0 + HEAD_DIM] = o.astype(o_ref.dtype)
        if not merged:
            l_ref[r0:r0 + SUB, :] = stats


def _band_bias(d):
    a = np.arange(SUB)[:, None]
    c = np.arange(WIN)[None, :]
    off = c - HALF_WIN - a
    slopes = (2.0 ** -np.arange(1, B_HEADS + 1)).astype(np.float32)
    bias = -slopes[:, None, None] * (np.abs(off) * d).astype(np.float32)[None]
    return np.where((np.abs(off) <= HALF_WIN)[None], bias, np.float32(NEG_INF)).astype(np.float32)


def _attn_b_pattern(src, cols, d, partials):
    b, _, seq_len, _ = src.shape
    col_q, col_k, col_v = cols
    tq = min(512, seq_len)
    nblk = seq_len // tq
    halo_per_blk = tq // HALF_WIN
    n_halo = seq_len // HALF_WIN
    n_hb = B_HEADS // HB_HEADS
    merged = tuple(dd for dd, _, _ in partials)
    bb = jnp.asarray(_band_bias(d))

    def main(col0, width=HB_W):
        return pl.BlockSpec((None, None, tq, width),
                            lambda bi, r, hb, i: (bi, r, i, col0 // width + hb))

    def left(col0):
        return pl.BlockSpec((None, None, HALF_WIN, HB_W),
                            lambda bi, r, hb, i: (bi, r, jnp.maximum(i * halo_per_blk - 1, 0),
                                                  col0 // HB_W + hb))

    def right(col0):
        return pl.BlockSpec((None, None, HALF_WIN, HB_W),
                            lambda bi, r, hb, i: (bi, r, jnp.minimum((i + 1) * halo_per_blk, n_halo - 1),
                                                  col0 // HB_W + hb))

    def part(dd, width):
        return pl.BlockSpec((None, dd, tq // dd, width), lambda bi, r, hb, i: (bi, 0, i, hb))

    in_specs = [main(col_q), left(col_k), main(col_k), right(col_k),
                left(col_v), main(col_v), right(col_v),
                pl.BlockSpec((HB_HEADS, SUB, WIN), lambda bi, r, hb, i: (hb, 0, 0))]
    args = [src] * 7 + [bb]
    scratch = [pltpu.VMEM((tq + 2 * HALF_WIN, HB_W), BF16)] * 2
    for dd, o_part, l_part in partials:
        in_specs += [part(dd, HB_W), part(dd, LANES)]
        args += [o_part, l_part]
        scratch += [pltpu.VMEM((HB_HEADS, tq, LANES), F32), pltpu.VMEM((tq, LANES), F32)]
    if merged:
        out_shape = jax.ShapeDtypeStruct((b, d, seq_len, D_B), BF16)
        out_specs = main(0)
    else:
        out_shape = (jax.ShapeDtypeStruct((b, d, seq_len, D_B), F32),
                     jax.ShapeDtypeStruct((b, d, seq_len, n_hb * LANES), F32))
        out_specs = (main(0), main(0, LANES))
    return pl.pallas_call(
        functools.partial(_attn_b_kernel, merged=merged, seq_len=seq_len),
        out_shape=out_shape,
        grid=(b, d, n_hb, nblk),
        in_specs=in_specs,
        out_specs=out_specs,
        scratch_shapes=scratch,
        compiler_params=_cparams(("parallel", "parallel", "parallel", "arbitrary")),
        name=f"attn_b_d{d}",
    )(*args)


def _attn_b(proj3, strided):
    b, t, _ = proj3.shape
    assert all(w // (2 * d) == HALF_WIN for w, d in DILATED_PATTERNS) and DILATED_PATTERNS[0][1] == 1
    partials = []
    for d, src in zip(STRIDES, strided):
        o_part, l_part = _attn_b_pattern(src, (0, D_B, 2 * D_B), d, [])
        partials.append((d, o_part, l_part))
    y = _attn_b_pattern(proj3.reshape(b, 1, t, D_IN), (COL_QB, COL_KB, COL_VB), 1, partials)
    return y.reshape(b, t, D_B)


def _gated(y_ref, g_refs, gain_ref):
    y = y_ref[...].astype(F32)
    ms = jnp.mean(y * y, axis=-1, keepdims=True)
    yn = y * lax.rsqrt(ms + EPS) * gain_ref[...]
    g = jnp.concatenate([r[...] for r in g_refs], axis=-1).astype(F32)
    return (yn * (g * jax.nn.sigmoid(g))).astype(BF16)


def _outproj_kernel(x_ref, ya_ref, yb_ref, ga0_ref, ga1_ref, gb0_ref, gb1_ref, na_ref, nb_ref,
                    w_ref, fn_ref, o_ref, *, final):
    za = _gated(ya_ref, (ga0_ref, ga1_ref), na_ref)
    zb = _gated(yb_ref, (gb0_ref, gb1_ref), nb_ref)
    out = x_ref[...]
    out = out + jnp.dot(za, w_ref[0:D_A, :], preferred_element_type=F32)
    out = out + jnp.dot(zb, w_ref[D_A:, :], preferred_element_type=F32)
    if final:
        ms = jnp.mean(out * out, axis=-1, keepdims=True)
        out = out * lax.rsqrt(ms + EPS) * fn_ref[...]
    o_ref[...] = out


def _outproj(x2, ya2, yb2, proj2, na, nb, w, fn, *, final, tm=512):
    m = x2.shape[0]
    half = D_A // 2

    def gate(col0, k):
        return pl.BlockSpec((tm, half), lambda i: (i, col0 // half + k))

    row = lambda width: pl.BlockSpec((tm, width), lambda i: (i, 0))
    const = lambda shape: pl.BlockSpec(shape, lambda i: (0, 0))
    return pl.pallas_call(
        functools.partial(_outproj_kernel, final=final),
        out_shape=jax.ShapeDtypeStruct((m, D_MODEL), F32),
        grid=(m // tm,),
        in_specs=[row(D_MODEL), row(D_A), row(D_B),
                  gate(COL_GA, 0), gate(COL_GA, 1), gate(COL_GB, 0), gate(COL_GB, 1),
                  const((1, D_A)), const((1, D_B)), const((D_A + D_B, D_MODEL)), const((1, D_MODEL))],
        out_specs=row(D_MODEL),
        compiler_params=_cparams(("parallel",)),
        name="outproj",
    )(x2, ya2, yb2, proj2, proj2, proj2, proj2, na, nb, w, fn)


def _rope_tables(t):
    pos = jnp.arange(t, dtype=jnp.int32)
    row = (pos // GRID_W).astype(F32)
    col = (pos % GRID_W).astype(F32)
    inv_freq = ROPE_THETA ** (-jnp.arange(0, AXIS_DIM, 2, dtype=F32) / AXIS_DIM)
    ang_r = row[:, None] * inv_freq[None, :]
    ang_c = col[:, None] * inv_freq[None, :]
    cos = jnp.concatenate([jnp.cos(ang_r)] * 2 + [jnp.cos(ang_c)] * 2, axis=-1)
    sin = jnp.concatenate([-jnp.sin(ang_r), jnp.sin(ang_r), -jnp.sin(ang_c), jnp.sin(ang_c)], axis=-1)
    return cos, sin


def kernel(x, norm_w, w_in, q_norm_a, k_norm_a, out_norm_a, out_norm_b, w_out, final_norm):
    b, t, d_model = x.shape
    depth = w_in.shape[0]
    assert d_model == D_MODEL and w_in.shape[1:] == (D_MODEL, D_IN)
    assert t % (DILATED_PATTERNS[-1][1] * SUB) == 0
    cos, sin = _rope_tables(t)
    w_in_bf = w_in.astype(BF16)
    w_out_bf = w_out.astype(BF16)
    x2 = x.reshape(b * t, D_MODEL)
    for l in range(depth):
        proj2, *strided = _inproj(x2, norm_w[l][None], w_in_bf[l], (q_norm_a[l] * (SCALE * LOG2E))[None],
                                  k_norm_a[l][None], cos, sin, seq=t)
        proj3 = proj2.reshape(b, t, D_IN)
        ya = _attn_a(proj3)
        yb = _attn_b(proj3, strided)
        x2 = _outproj(x2, ya.reshape(b * t, D_A), yb.reshape(b * t, D_B), proj2,
                      out_norm_a[l][None], out_norm_b[l][None], w_out_bf[l], final_norm[None],
                      final=l == depth - 1)
    return x2.reshape(b, t, D_MODEL)
```

```python
import functools

import numpy as np
import jax
import jax.numpy as jnp
from jax import lax
from jax.experimental import pallas as pl
from jax.experimental.pallas import tpu as pltpu

D_MODEL = 2048
HEAD_DIM = 128
A_HEADS = 8
A_KV_HEADS = 2
A_GROUP = A_HEADS // A_KV_HEADS
B_HEADS = 8
D_A = A_HEADS * HEAD_DIM
D_B = B_HEADS * HEAD_DIM
KV_DIM = A_KV_HEADS * HEAD_DIM
D_IN = 2 * D_A + 2 * KV_DIM + 4 * D_B
GRID_W = 64
AXIS_DIM = HEAD_DIM // 2
ROPE_THETA = 10000.0
DILATED_PATTERNS = ((128, 1), (512, 4), (2048, 16))
HALF_WIN = 64
SCALE = HEAD_DIM ** -0.5
LOG2E = float(np.log2(np.e))
EPS = 1e-6
NEG_INF = -1e30

COL_QA, COL_KA, COL_VA, COL_GA = 0, D_A, D_A + KV_DIM, D_A + 2 * KV_DIM
COL_QB = COL_GA + D_A
COL_KB, COL_VB, COL_GB = COL_QB + D_B, COL_QB + 2 * D_B, COL_QB + 3 * D_B

LANES = 128
VMEM_LIMIT = 56 * 1024 * 1024

BF16 = jnp.bfloat16
F32 = jnp.float32


def _cparams(sem):
    return pltpu.CompilerParams(dimension_semantics=sem, vmem_limit_bytes=VMEM_LIMIT)


def _rope_group(a, gain, cos, sin):
    ms = jnp.mean(a * a, axis=-1, keepdims=True)
    y = a * lax.rsqrt(ms + EPS) * gain
    lane = lax.broadcasted_iota(jnp.int32, y.shape, 1)
    partner = jnp.where((lane % AXIS_DIM) < AXIS_DIM // 2,
                        pltpu.roll(y, LANES - AXIS_DIM // 2, 1), pltpu.roll(y, AXIS_DIM // 2, 1))
    return y * cos + partner * sin


STRIDES = tuple(d for _, d in DILATED_PATTERNS if d > 1)
QKVB_W = COL_GB - COL_QB
ROW_CHUNK = 256


def _inproj_kernel(x_ref, nw_ref, w_ref, qg_ref, kg_ref, cos_ref, sin_ref, o_ref, *rest, tn):
    strided_refs, (h_ref, slab_ref, slab2_ref) = rest[:len(STRIDES)], rest[len(STRIDES):]
    j = pl.program_id(1)

    @pl.when(j == 0)
    def _():
        x = x_ref[...]
        ms = jnp.mean(x * x, axis=-1, keepdims=True)
        h_ref[...] = (x * lax.rsqrt(ms + EPS) * nw_ref[...]).astype(BF16)

    tm = x_ref.shape[0]
    n_chunks = tm // ROW_CHUNK
    groups_per_tile = tn // HEAD_DIM
    n_q, n_k = D_A // HEAD_DIM, KV_DIM // HEAD_DIM
    n_special = -(-(n_q + n_k) // groups_per_tile)
    first_b, end_b = COL_QB // tn, COL_GB // tn

    def tile_body(kinds, strided):
        for ch in range(n_chunks):
            rows = slice(ch * ROW_CHUNK, (ch + 1) * ROW_CHUNK)
            acc = jnp.dot(h_ref[rows, :], w_ref[...], preferred_element_type=F32)
            for c, kind in enumerate(kinds):
                a = acc[:, c * HEAD_DIM:(c + 1) * HEAD_DIM]
                if kind is not None:
                    gain = qg_ref[...] if kind == 'q' else kg_ref[...]
                    a = _rope_group(a, gain, cos_ref[rows, :], sin_ref[rows, :])
                o_ref[rows, c * HEAD_DIM:(c + 1) * HEAD_DIM] = a.astype(o_ref.dtype)
                if strided:
                    slab_ref[ch, c] = a
            if strided:
                (d1, ref1), (d2, ref2) = zip(STRIDES, strided_refs)
                n1, n2, step = ROW_CHUNK // d1, ROW_CHUNK // d2, d2 // d1
                for c in range(groups_per_tile):
                    cols = slice(c * LANES, (c + 1) * LANES)
                    for r1 in range(d1):
                        v = slab_ref[ch, c, pl.ds(r1, n1, stride=d1), :]
                        ref1[r1, ch * n1:(ch + 1) * n1, cols] = v.astype(ref1.dtype)
                        slab2_ref[ch, c, r1] = v
                    for r1 in range(d1):
                        for q2 in range(step):
                            v = slab2_ref[ch, c, r1, pl.ds(q2, n2, stride=step), :]
                            ref2[q2 * d1 + r1, ch * n2:(ch + 1) * n2, cols] = v.astype(ref2.dtype)

    def kinds_of(tile):
        gs = [tile * groups_per_tile + c for c in range(groups_per_tile)]
        return tuple('q' if g < n_q else 'k' if g < n_q + n_k else None for g in gs)

    special = {}
    for tile in range(n_special):
        special.setdefault(kinds_of(tile), []).append(tile)
    for kinds, tiles in special.items():
        pl.when((j >= tiles[0]) & (j <= tiles[-1]))(functools.partial(tile_body, kinds, False))
    plain = (None,) * groups_per_tile
    pl.when((j >= first_b) & (j < end_b))(functools.partial(tile_body, plain, True))
    pl.when((j >= n_special) & ((j < first_b) | (j >= end_b)))(functools.partial(tile_body, plain, False))


def _inproj(x2, nw, w, qg, kg, cos, sin, *, seq, tm=1024, tn=512):
    m = x2.shape[0]
    t_blocks = seq // tm
    batch = m // seq
    assert COL_QB % tn == 0 and COL_GB % tn == 0
    assert len(STRIDES) == 2 and STRIDES[1] % STRIDES[0] == 0 and ROW_CHUNK % STRIDES[1] == 0
    first_b, n_b = COL_QB // tn, QKVB_W // tn

    def strided_spec(d):
        return pl.BlockSpec((None, d, tm // d, tn),
                            lambda i, j: (i // t_blocks, 0, i % t_blocks, jnp.clip(j - first_b, 0, n_b - 1)))

    return pl.pallas_call(
        functools.partial(_inproj_kernel, tn=tn),
        out_shape=(jax.ShapeDtypeStruct((m, D_IN), BF16),)
        + tuple(jax.ShapeDtypeStruct((batch, d, seq // d, QKVB_W), BF16) for d in STRIDES),
        grid=(m // tm, D_IN // tn),
        in_specs=[
            pl.BlockSpec((tm, D_MODEL), lambda i, j: (i, 0)),
            pl.BlockSpec((1, D_MODEL), lambda i, j: (0, 0)),
            pl.BlockSpec((D_MODEL, tn), lambda i, j: (0, j)),
            pl.BlockSpec((1, HEAD_DIM), lambda i, j: (0, 0)),
            pl.BlockSpec((1, HEAD_DIM), lambda i, j: (0, 0)),
            pl.BlockSpec((tm, HEAD_DIM), lambda i, j: (i % t_blocks, 0)),
            pl.BlockSpec((tm, HEAD_DIM), lambda i, j: (i % t_blocks, 0)),
        ],
        out_specs=(pl.BlockSpec((tm, tn), lambda i, j: (i, j)),) + tuple(strided_spec(d) for d in STRIDES),
        scratch_shapes=[pltpu.VMEM((tm, D_MODEL), BF16),
                        pltpu.VMEM((tm // ROW_CHUNK, tn // LANES, ROW_CHUNK, LANES), F32),
                        pltpu.VMEM((tm // ROW_CHUNK, tn // LANES, STRIDES[0], ROW_CHUNK // STRIDES[0], LANES), F32)],
        compiler_params=_cparams(("arbitrary", "arbitrary")),
        name="inproj",
    )(x2, nw, w, qg, kg, cos, sin)


ONES_ROWS = 16
KV_UNROLL = 4


def _attn_a_kernel(q_ref, k_ref, v_ref, o_ref, vt_ref, s0_ref, s1_ref, acc_ref, qt_ref, *, tk):
    tq = q_ref.shape[0]
    n_kv = k_ref.shape[0] // tk

    @pl.when(pl.program_id(2) == 0)
    def _():
        for kk in range(n_kv):
            vt_ref[kk, 0:HEAD_DIM, :] = v_ref[kk * tk:(kk + 1) * tk, :].astype(F32).T.astype(BF16)
            vt_ref[kk, HEAD_DIM:, :] = jnp.ones((ONES_ROWS, tk), BF16)

    sbuf = (s0_ref, s1_ref)
    for hh in range(A_GROUP):
        qt_ref[hh] = q_ref[:, hh * HEAD_DIM:(hh + 1) * HEAD_DIM].astype(F32).T.astype(BF16)

    def scores(kk, dst):
        start = pl.multiple_of(kk * tk, tk)
        k = k_ref[pl.ds(start, tk), :]
        for hh in range(A_GROUP):
            dst[hh] = jnp.dot(k, qt_ref[hh], preferred_element_type=F32)

    def softmax_pv(kk, src, ml):
        vt = vt_ref[kk]
        new = []
        for hh in range(A_GROUP):
            m, l = ml[hh]
            s = src[hh]
            m_new = jnp.maximum(m, jnp.max(s, axis=0, keepdims=True))
            alpha = jnp.exp2(m - m_new)
            p = jnp.exp2(s - m_new).astype(BF16)
            pv = jnp.dot(vt, p, preferred_element_type=F32)
            acc_ref[hh] = alpha * acc_ref[hh] + pv[0:HEAD_DIM]
            new.append((m_new, alpha * l + pv[HEAD_DIM:HEAD_DIM + 1]))
        return tuple(new)

    def run(kk0, ml, n_tiles, n_scores):
        for u in range(n_tiles):
            if u < n_scores:
                scores(kk0 + u + 1, sbuf[(u + 1) % 2])
            ml = softmax_pv(kk0 + u, sbuf[u % 2], ml)
        return ml

    acc_ref[...] = jnp.zeros_like(acc_ref)
    scores(0, sbuf[0])
    ml = tuple((jnp.full((1, tq), -jnp.inf, F32), jnp.zeros((1, tq), F32)) for _ in range(A_GROUP))
    ml = lax.fori_loop(0, n_kv // KV_UNROLL - 1,
                       lambda jj, ml: run(jj * KV_UNROLL, ml, KV_UNROLL, KV_UNROLL), ml)
    ml = run(n_kv - KV_UNROLL, ml, KV_UNROLL, KV_UNROLL - 1)
    for hh in range(A_GROUP):
        o_ref[:, hh * HEAD_DIM:(hh + 1) * HEAD_DIM] = (acc_ref[hh] / ml[hh][1]).T.astype(o_ref.dtype)


def _attn_a(proj3, *, tq=256, tk=512):
    b, t, _ = proj3.shape
    gw = A_GROUP * HEAD_DIM
    return pl.pallas_call(
        functools.partial(_attn_a_kernel, tk=tk),
        out_shape=jax.ShapeDtypeStruct((b, t, D_A), BF16),
        grid=(b, A_KV_HEADS, t // tq),
        in_specs=[
            pl.BlockSpec((None, tq, gw), lambda bi, g, qi: (bi, qi, COL_QA // gw + g)),
            pl.BlockSpec((None, t, HEAD_DIM), lambda bi, g, qi: (bi, 0, COL_KA // HEAD_DIM + g)),
            pl.BlockSpec((None, t, HEAD_DIM), lambda bi, g, qi: (bi, 0, COL_VA // HEAD_DIM + g)),
        ],
        out_specs=pl.BlockSpec((None, tq, gw), lambda bi, g, qi: (bi, qi, g)),
        scratch_shapes=[pltpu.VMEM((t // tk, HEAD_DIM + ONES_ROWS, tk), BF16),
                        pltpu.VMEM((A_GROUP, tk, tq), F32), pltpu.VMEM((A_GROUP, tk, tq), F32),
                        pltpu.VMEM((A_GROUP, HEAD_DIM, tq), F32), pltpu.VMEM((A_GROUP, HEAD_DIM, tq), BF16)],
        compiler_params=_cparams(("parallel", "parallel", "arbitrary")),
        name="attn_a",
    )(proj3, proj3, proj3)


HB_HEADS = 4
HB_W = HB_HEADS * HEAD_DIM
SUB = 128
WIN = SUB + 2 * HALF_WIN
LSE_REP = LANES // HB_HEADS


def _attn_b_kernel(*refs, merged, seq_len):
    it = iter(refs)
    q_ref, kl_ref, km_ref, kr_ref, vl_ref, vm_ref, vr_ref, bb_ref = (next(it) for _ in range(8))
    parts = [(next(it), next(it)) for _ in merged]
    o_ref = next(it)
    l_ref = None if merged else next(it)
    kbuf, vbuf = next(it), next(it)
    flat = [(next(it), next(it)) for _ in merged]

    tq = q_ref.shape[0]
    n_sub = tq // SUB
    i = pl.program_id(3)
    kbuf[0:HALF_WIN, :] = kl_ref[...]
    kbuf[HALF_WIN:HALF_WIN + tq, :] = km_ref[...]
    kbuf[HALF_WIN + tq:, :] = kr_ref[...]
    ones = jnp.ones((tq + 2 * HALF_WIN, HEAD_DIM), BF16)
    for h in range(HB_HEADS):
        c0, w0 = h * HEAD_DIM, 2 * h * HEAD_DIM
        vbuf[0:HALF_WIN, w0:w0 + HEAD_DIM] = vl_ref[:, c0:c0 + HEAD_DIM]
        vbuf[HALF_WIN:HALF_WIN + tq, w0:w0 + HEAD_DIM] = vm_ref[:, c0:c0 + HEAD_DIM]
        vbuf[HALF_WIN + tq:, w0:w0 + HEAD_DIM] = vr_ref[:, c0:c0 + HEAD_DIM]
        vbuf[:, w0 + HEAD_DIM:w0 + 2 * HEAD_DIM] = ones

    for d, (po_ref, pl_ref), (fo_ref, fl_ref) in zip(merged, parts, flat):
        for r in range(d):
            fl_ref[pl.ds(r, tq // d, stride=d), :] = pl_ref[r]
            for h in range(HB_HEADS):
                fo_ref[h, pl.ds(r, tq // d, stride=d), :] = (
                    po_ref[r, :, h * HEAD_DIM:(h + 1) * HEAD_DIM].astype(F32))

    col = lax.broadcasted_iota(jnp.int32, (1, WIN), 1)
    lane_head = lax.broadcasted_iota(jnp.int32, (SUB, LANES), 1) // LSE_REP
    for sub in range(n_sub):
        r0 = sub * SUB
        kidx = i * tq + (r0 - HALF_WIN) + col
        colmask = jnp.where((kidx >= 0) & (kidx < seq_len), 0.0, NEG_INF).astype(F32)
        stats = jnp.zeros((SUB, LANES), F32)
        for h in range(HB_HEADS):
            c0 = h * HEAD_DIM
            q = (q_ref[r0:r0 + SUB, c0:c0 + HEAD_DIM].astype(F32) * (SCALE * LOG2E)).astype(BF16)
            k = kbuf[r0:r0 + WIN, c0:c0 + HEAD_DIM]
            s = lax.dot_general(q, k, (((1,), (1,)), ((), ())), preferred_element_type=F32)
            s = s + bb_ref[h]
            if sub == 0 or sub == n_sub - 1:
                s = s + colmask
            m = jnp.max(s, axis=-1, keepdims=True)
            p = jnp.exp2(s - m).astype(BF16)
            pv = jnp.dot(p, vbuf[r0:r0 + WIN, 2 * c0:2 * c0 + 2 * HEAD_DIM], preferred_element_type=F32)
            l = pv[:, HEAD_DIM:]
            o = pv[:, 0:HEAD_DIM] / l
            lse = m + jnp.log2(l)
            if merged:
                lses = [lse] + [fl_ref[r0:r0 + SUB, h * LSE_REP:h * LSE_REP + 1] for _, fl_ref in flat]
                outs = [o] + [fo_ref[h, r0:r0 + SUB, :] for fo_ref, _ in flat]
                top = functools.reduce(jnp.maximum, lses)
                ws = [jnp.exp2(x - top) for x in lses]
                o = sum(w * x for w, x in zip(ws, outs)) / sum(ws)
            else:
                stats = jnp.where(lane_head == h, lse, stats)
            o_ref[r0:r0 + SUB, c0:c0 + HEAD_DIM] = o.astype(o_ref.dtype)
        if not merged:
            l_ref[r0:r0 + SUB, :] = stats


def _band_bias(d):
    a = np.arange(SUB)[:, None]
    c = np.arange(WIN)[None, :]
    off = c - HALF_WIN - a
    slopes = (2.0 ** -np.arange(1, B_HEADS + 1)).astype(np.float32)
    bias = -slopes[:, None, None] * (np.abs(off) * d).astype(np.float32)[None] * np.float32(LOG2E)
    return np.where((np.abs(off) <= HALF_WIN)[None], bias, np.float32(NEG_INF)).astype(np.float32)


def _attn_b_pattern(src, cols, d, partials):
    b, _, seq_len, _ = src.shape
    col_q, col_k, col_v = cols
    tq = min(512, seq_len)
    nblk = seq_len // tq
    halo_per_blk = tq // HALF_WIN
    n_halo = seq_len // HALF_WIN
    n_hb = B_HEADS // HB_HEADS
    merged = tuple(dd for dd, _, _ in partials)
    bb = jnp.asarray(_band_bias(d))

    def main(col0, width=HB_W):
        return pl.BlockSpec((None, None, tq, width),
                            lambda bi, r, hb, i: (bi, r, i, col0 // width + hb))

    def left(col0):
        return pl.BlockSpec((None, None, HALF_WIN, HB_W),
                            lambda bi, r, hb, i: (bi, r, jnp.maximum(i * halo_per_blk - 1, 0),
                                                  col0 // HB_W + hb))

    def right(col0):
        return pl.BlockSpec((None, None, HALF_WIN, HB_W),
                            lambda bi, r, hb, i: (bi, r, jnp.minimum((i + 1) * halo_per_blk, n_halo - 1),
                                                  col0 // HB_W + hb))

    def part(dd, width):
        return pl.BlockSpec((None, dd, tq // dd, width), lambda bi, r, hb, i: (bi, 0, i, hb))

    in_specs = [main(col_q), left(col_k), main(col_k), right(col_k),
                left(col_v), main(col_v), right(col_v),
                pl.BlockSpec((HB_HEADS, SUB, WIN), lambda bi, r, hb, i: (hb, 0, 0))]
    args = [src] * 7 + [bb]
    scratch = [pltpu.VMEM((tq + 2 * HALF_WIN, HB_W), BF16), pltpu.VMEM((tq + 2 * HALF_WIN, 2 * HB_W), BF16)]
    for dd, o_part, l_part in partials:
        in_specs += [part(dd, HB_W), part(dd, LANES)]
        args += [o_part, l_part]
        scratch += [pltpu.VMEM((HB_HEADS, tq, LANES), F32), pltpu.VMEM((tq, LANES), F32)]
    if merged:
        out_shape = jax.ShapeDtypeStruct((b, d, seq_len, D_B), BF16)
        out_specs = main(0)
    else:
        out_shape = (jax.ShapeDtypeStruct((b, d, seq_len, D_B), BF16),
                     jax.ShapeDtypeStruct((b, d, seq_len, n_hb * LANES), F32))
        out_specs = (main(0), main(0, LANES))
    return pl.pallas_call(
        functools.partial(_attn_b_kernel, merged=merged, seq_len=seq_len),
        out_shape=out_shape,
        grid=(b, d, n_hb, nblk),
        in_specs=in_specs,
        out_specs=out_specs,
        scratch_shapes=scratch,
        compiler_params=_cparams(("parallel", "parallel", "parallel", "arbitrary")),
        name=f"attn_b_d{d}",
    )(*args)


def _attn_b(proj3, strided):
    b, t, _ = proj3.shape
    assert all(w // (2 * d) == HALF_WIN for w, d in DILATED_PATTERNS) and DILATED_PATTERNS[0][1] == 1
    partials = []
    for d, src in zip(STRIDES, strided):
        o_part, l_part = _attn_b_pattern(src, (0, D_B, 2 * D_B), d, [])
        partials.append((d, o_part, l_part))
    y = _attn_b_pattern(proj3.reshape(b, 1, t, D_IN), (COL_QB, COL_KB, COL_VB), 1, partials)
    return y.reshape(b, t, D_B)


def _gated(y_ref, g_refs, gain_ref):
    y = y_ref[...].astype(F32)
    ms = jnp.mean(y * y, axis=-1, keepdims=True)
    yn = y * lax.rsqrt(ms + EPS) * gain_ref[...]
    g = jnp.concatenate([r[...] for r in g_refs], axis=-1).astype(F32)
    return (yn * (g * jax.nn.sigmoid(g))).astype(BF16)


def _outproj_kernel(x_ref, ya_ref, yb_ref, ga0_ref, ga1_ref, gb0_ref, gb1_ref, na_ref, nb_ref,
                    w_ref, fn_ref, o_ref, *, final):
    za = _gated(ya_ref, (ga0_ref, ga1_ref), na_ref)
    zb = _gated(yb_ref, (gb0_ref, gb1_ref), nb_ref)
    out = x_ref[...]
    out = out + jnp.dot(za, w_ref[0:D_A, :], preferred_element_type=F32)
    out = out + jnp.dot(zb, w_ref[D_A:, :], preferred_element_type=F32)
    if final:
        ms = jnp.mean(out * out, axis=-1, keepdims=True)
        out = out * lax.rsqrt(ms + EPS) * fn_ref[...]
    o_ref[...] = out


def _outproj(x2, ya2, yb2, proj2, na, nb, w, fn, *, final, tm=512):
    m = x2.shape[0]
    half = D_A // 2

    def gate(col0, k):
        return pl.BlockSpec((tm, half), lambda i: (i, col0 // half + k))

    row = lambda width: pl.BlockSpec((tm, width), lambda i: (i, 0))
    const = lambda shape: pl.BlockSpec(shape, lambda i: (0, 0))
    return pl.pallas_call(
        functools.partial(_outproj_kernel, final=final),
        out_shape=jax.ShapeDtypeStruct((m, D_MODEL), F32),
        grid=(m // tm,),
        in_specs=[row(D_MODEL), row(D_A), row(D_B),
                  gate(COL_GA, 0), gate(COL_GA, 1), gate(COL_GB, 0), gate(COL_GB, 1),
                  const((1, D_A)), const((1, D_B)), const((D_A + D_B, D_MODEL)), const((1, D_MODEL))],
        out_specs=row(D_MODEL),
        compiler_params=_cparams(("parallel",)),
        name="outproj",
    )(x2, ya2, yb2, proj2, proj2, proj2, proj2, na, nb, w, fn)


def _rope_tables(t):
    pos = jnp.arange(t, dtype=jnp.int32)
    row = (pos // GRID_W).astype(F32)
    col = (pos % GRID_W).astype(F32)
    inv_freq = ROPE_THETA ** (-jnp.arange(0, AXIS_DIM, 2, dtype=F32) / AXIS_DIM)
    ang_r = row[:, None] * inv_freq[None, :]
    ang_c = col[:, None] * inv_freq[None, :]
    cos = jnp.concatenate([jnp.cos(ang_r)] * 2 + [jnp.cos(ang_c)] * 2, axis=-1)
    sin = jnp.concatenate([-jnp.sin(ang_r), jnp.sin(ang_r), -jnp.sin(ang_c), jnp.sin(ang_c)], axis=-1)
    return cos, sin


def kernel(x, norm_w, w_in, q_norm_a, k_norm_a, out_norm_a, out_norm_b, w_out, final_norm):
    b, t, d_model = x.shape
    depth = w_in.shape[0]
    assert d_model == D_MODEL and w_in.shape[1:] == (D_MODEL, D_IN)
    assert t % (DILATED_PATTERNS[-1][1] * SUB) == 0
    cos, sin = _rope_tables(t)
    w_in_bf = w_in.astype(BF16)
    w_out_bf = w_out.astype(BF16)
    x2 = x.reshape(b * t, D_MODEL)
    for l in range(depth):
        proj2, *strided = _inproj(x2, norm_w[l][None], w_in_bf[l], (q_norm_a[l] * (SCALE * LOG2E))[None],
                                  k_norm_a[l][None], cos, sin, seq=t)
        proj3 = proj2.reshape(b, t, D_IN)
        ya = _attn_a(proj3)
        yb = _attn_b(proj3, strided)
        x2 = _outproj(x2, ya.reshape(b * t, D_A), yb.reshape(b * t, D_B), proj2,
                      out_norm_a[l][None], out_norm_b[l][None], w_out_bf[l], final_norm[None],
                      final=l == depth - 1)
    return x2.reshape(b, t, D_MODEL)
```

```python
import functools

import numpy as np
import jax
import jax.numpy as jnp
from jax import lax
from jax.experimental import pallas as pl
from jax.experimental.pallas import tpu as pltpu

D_MODEL = 2048
HEAD_DIM = 128
A_HEADS = 8
A_KV_HEADS = 2
A_GROUP = A_HEADS // A_KV_HEADS
B_HEADS = 8
D_A = A_HEADS * HEAD_DIM
D_B = B_HEADS * HEAD_DIM
KV_DIM = A_KV_HEADS * HEAD_DIM
D_IN = 2 * D_A + 2 * KV_DIM + 4 * D_B
GRID_W = 64
AXIS_DIM = HEAD_DIM // 2
ROPE_THETA = 10000.0
DILATED_PATTERNS = ((128, 1), (512, 4), (2048, 16))
HALF_WIN = 64
SCALE = HEAD_DIM ** -0.5
LOG2E = float(np.log2(np.e))
EPS = 1e-6
NEG_INF = -1e30

COL_QA, COL_KA, COL_VA, COL_GA = 0, D_A, D_A + KV_DIM, D_A + 2 * KV_DIM
COL_QB = COL_GA + D_A
COL_KB, COL_VB, COL_GB = COL_QB + D_B, COL_QB + 2 * D_B, COL_QB + 3 * D_B

LANES = 128
VMEM_LIMIT = 56 * 1024 * 1024

BF16 = jnp.bfloat16
F32 = jnp.float32


def _cparams(sem):
    return pltpu.CompilerParams(dimension_semantics=sem, vmem_limit_bytes=VMEM_LIMIT)


def _rope_group(a, gain, cos, sin):
    ms = jnp.mean(a * a, axis=-1, keepdims=True)
    y = a * lax.rsqrt(ms + EPS) * gain
    lane = lax.broadcasted_iota(jnp.int32, y.shape, 1)
    partner = jnp.where((lane % AXIS_DIM) < AXIS_DIM // 2,
                        pltpu.roll(y, LANES - AXIS_DIM // 2, 1), pltpu.roll(y, AXIS_DIM // 2, 1))
    return y * cos + partner * sin


STRIDES = tuple(d for _, d in DILATED_PATTERNS if d > 1)
QKVB_W = COL_GB - COL_QB
ROW_CHUNK = 256
INPROJ_TN = 512
N_SLABS = 2


def _inproj_kernel(h_ref, w_ref, qg_ref, kg_ref, cos_ref, sin_ref, o_ref, *rest, tn):
    strided_refs, (slab_ref, slab2_ref) = rest[:len(STRIDES)], rest[len(STRIDES):]
    j = pl.program_id(1)
    tm = h_ref.shape[0]
    n_chunks = tm // ROW_CHUNK
    groups_per_tile = tn // HEAD_DIM
    n_q, n_k = D_A // HEAD_DIM, KV_DIM // HEAD_DIM
    n_special = -(-(n_q + n_k) // groups_per_tile)
    first_b, end_b = COL_QB // tn, COL_GB // tn

    def tile_body(kinds, strided):
        for ch in range(n_chunks):
            rows = slice(ch * ROW_CHUNK, (ch + 1) * ROW_CHUNK)
            acc = jnp.dot(h_ref[rows, :], w_ref[...], preferred_element_type=F32)
            for c, kind in enumerate(kinds):
                a = acc[:, c * HEAD_DIM:(c + 1) * HEAD_DIM]
                if kind is not None:
                    gain = qg_ref[...] if kind == 'q' else kg_ref[...]
                    a = _rope_group(a, gain, cos_ref[rows, :], sin_ref[rows, :])
                o_ref[rows, c * HEAD_DIM:(c + 1) * HEAD_DIM] = a.astype(o_ref.dtype)
                if strided:
                    slab_ref[ch % N_SLABS, c] = a
            if strided:
                (d1, ref1), (d2, ref2) = zip(STRIDES, strided_refs)
                n1, n2, step = ROW_CHUNK // d1, ROW_CHUNK // d2, d2 // d1
                for c in range(groups_per_tile):
                    cols = slice(c * LANES, (c + 1) * LANES)
                    for r1 in range(d1):
                        v = slab_ref[ch % N_SLABS, c, pl.ds(r1, n1, stride=d1), :]
                        ref1[r1, ch * n1:(ch + 1) * n1, cols] = v.astype(ref1.dtype)
                        slab2_ref[ch % N_SLABS, c, r1] = v
                    for r1 in range(d1):
                        for q2 in range(step):
                            v = slab2_ref[ch % N_SLABS, c, r1, pl.ds(q2, n2, stride=step), :]
                            ref2[q2 * d1 + r1, ch * n2:(ch + 1) * n2, cols] = v.astype(ref2.dtype)

    def kinds_of(tile):
        gs = [tile * groups_per_tile + c for c in range(groups_per_tile)]
        return tuple('q' if g < n_q else 'k' if g < n_q + n_k else None for g in gs)

    special = {}
    for tile in range(n_special):
        special.setdefault(kinds_of(tile), []).append(tile)
    for kinds, tiles in special.items():
        pl.when((j >= tiles[0]) & (j <= tiles[-1]))(functools.partial(tile_body, kinds, False))
    plain = (None,) * groups_per_tile
    pl.when((j >= first_b) & (j < end_b))(functools.partial(tile_body, plain, True))
    pl.when((j >= n_special) & ((j < first_b) | (j >= end_b)))(functools.partial(tile_body, plain, False))


def _norm_kernel(x_ref, nw_ref, h_ref):
    x = x_ref[...]
    ms = jnp.mean(x * x, axis=-1, keepdims=True)
    h_ref[...] = (x * lax.rsqrt(ms + EPS) * nw_ref[...]).astype(h_ref.dtype)


def _norm(x2, nw, *, tm=1024):
    m = x2.shape[0]
    return pl.pallas_call(
        _norm_kernel,
        out_shape=jax.ShapeDtypeStruct((m, D_MODEL), BF16),
        grid=(m // tm,),
        in_specs=[pl.BlockSpec((tm, D_MODEL), lambda i: (i, 0)), pl.BlockSpec((1, D_MODEL), lambda i: (0, 0))],
        out_specs=pl.BlockSpec((tm, D_MODEL), lambda i: (i, 0)),
        compiler_params=_cparams(("parallel",)),
        name="norm",
    )(x2, nw)


def _inproj(h2, w, qg, kg, cos, sin, *, layer, seq, tm=2048):
    m = h2.shape[0]
    tn = w.shape[-1]
    tm = min(tm, seq)
    t_blocks = seq // tm
    batch = m // seq
    assert COL_QB % tn == 0 and COL_GB % tn == 0
    assert len(STRIDES) == 2 and STRIDES[1] % STRIDES[0] == 0 and ROW_CHUNK % STRIDES[1] == 0
    first_b, n_b = COL_QB // tn, QKVB_W // tn

    def strided_spec(d):
        return pl.BlockSpec((None, d, tm // d, tn),
                            lambda i, j: (i // t_blocks, 0, i % t_blocks, jnp.clip(j - first_b, 0, n_b - 1)))

    return pl.pallas_call(
        functools.partial(_inproj_kernel, tn=tn),
        out_shape=(jax.ShapeDtypeStruct((m, D_IN), BF16),)
        + tuple(jax.ShapeDtypeStruct((batch, d, seq // d, QKVB_W), BF16) for d in STRIDES),
        grid=(m // tm, D_IN // tn),
        in_specs=[
            pl.BlockSpec((tm, D_MODEL), lambda i, j: (i, 0)),
            pl.BlockSpec((None, None, D_MODEL, tn), lambda i, j: (layer, j, 0, 0)),
            pl.BlockSpec((1, HEAD_DIM), lambda i, j: (0, 0)),
            pl.BlockSpec((1, HEAD_DIM), lambda i, j: (0, 0)),
            pl.BlockSpec((tm, HEAD_DIM), lambda i, j: (i % t_blocks, 0)),
            pl.BlockSpec((tm, HEAD_DIM), lambda i, j: (i % t_blocks, 0)),
        ],
        out_specs=(pl.BlockSpec((tm, tn), lambda i, j: (i, j)),) + tuple(strided_spec(d) for d in STRIDES),
        scratch_shapes=[pltpu.VMEM((N_SLABS, tn // LANES, ROW_CHUNK, LANES), F32),
                        pltpu.VMEM((N_SLABS, tn // LANES, STRIDES[0], ROW_CHUNK // STRIDES[0], LANES), F32)],
        compiler_params=_cparams(("arbitrary", "arbitrary")),
        name="inproj",
    )(h2, w, qg, kg, cos, sin)


ONES_ROWS = 16
KV_UNROLL = 4


def _attn_a_kernel(q_ref, k_ref, v_ref, o_ref, vt_ref, s0_ref, s1_ref, acc_ref, qt_ref, *, tk):
    tq = q_ref.shape[0]
    n_kv = k_ref.shape[0] // tk

    @pl.when(pl.program_id(2) == 0)
    def _():
        for kk in range(n_kv):
            vt_ref[kk, 0:HEAD_DIM, :] = v_ref[kk * tk:(kk + 1) * tk, :].astype(F32).T.astype(BF16)
            vt_ref[kk, HEAD_DIM:, :] = jnp.ones((ONES_ROWS, tk), BF16)

    sbuf = (s0_ref, s1_ref)
    for hh in range(A_GROUP):
        qt_ref[hh] = q_ref[:, hh * HEAD_DIM:(hh + 1) * HEAD_DIM].astype(F32).T.astype(BF16)

    def scores(kk, dst):
        start = pl.multiple_of(kk * tk, tk)
        k = k_ref[pl.ds(start, tk), :]
        for hh in range(A_GROUP):
            dst[hh] = jnp.dot(k, qt_ref[hh], preferred_element_type=F32)

    def softmax_pv(kk, src, ml):
        vt = vt_ref[kk]
        new = []
        for hh in range(A_GROUP):
            m, l = ml[hh]
            s = src[hh]
            m_new = jnp.maximum(m, jnp.max(s, axis=0, keepdims=True))
            alpha = jnp.exp2(m - m_new)
            p = jnp.exp2(s - m_new).astype(BF16)
            pv = jnp.dot(vt, p, preferred_element_type=F32)
            acc_ref[hh] = alpha * acc_ref[hh] + pv[0:HEAD_DIM]
            new.append((m_new, alpha * l + pv[HEAD_DIM:HEAD_DIM + 1]))
        return tuple(new)

    def run(kk0, ml, n_tiles, n_scores):
        for u in range(n_tiles):
            if u < n_scores:
                scores(kk0 + u + 1, sbuf[(u + 1) % 2])
            ml = softmax_pv(kk0 + u, sbuf[u % 2], ml)
        return ml

    acc_ref[...] = jnp.zeros_like(acc_ref)
    scores(0, sbuf[0])
    ml = tuple((jnp.full((1, tq), -jnp.inf, F32), jnp.zeros((1, tq), F32)) for _ in range(A_GROUP))
    unroll = min(KV_UNROLL, n_kv)
    ml = lax.fori_loop(0, n_kv // unroll - 1, lambda jj, ml: run(jj * unroll, ml, unroll, unroll), ml)
    ml = run(n_kv - unroll, ml, unroll, unroll - 1)
    for hh in range(A_GROUP):
        o_ref[:, hh * HEAD_DIM:(hh + 1) * HEAD_DIM] = (acc_ref[hh] / ml[hh][1]).T.astype(o_ref.dtype)


def _attn_a(proj3, *, tq=256, tk=512):
    b, t, _ = proj3.shape
    gw = A_GROUP * HEAD_DIM
    return pl.pallas_call(
        functools.partial(_attn_a_kernel, tk=tk),
        out_shape=jax.ShapeDtypeStruct((b, t, D_A), BF16),
        grid=(b, A_KV_HEADS, t // tq),
        in_specs=[
            pl.BlockSpec((None, tq, gw), lambda bi, g, qi: (bi, qi, COL_QA // gw + g)),
            pl.BlockSpec((None, t, HEAD_DIM), lambda bi, g, qi: (bi, 0, COL_KA // HEAD_DIM + g)),
            pl.BlockSpec((None, t, HEAD_DIM), lambda bi, g, qi: (bi, 0, COL_VA // HEAD_DIM + g)),
        ],
        out_specs=pl.BlockSpec((None, tq, gw), lambda bi, g, qi: (bi, qi, g)),
        scratch_shapes=[pltpu.VMEM((t // tk, HEAD_DIM + ONES_ROWS, tk), BF16),
                        pltpu.VMEM((A_GROUP, tk, tq), F32), pltpu.VMEM((A_GROUP, tk, tq), F32),
                        pltpu.VMEM((A_GROUP, HEAD_DIM, tq), F32), pltpu.VMEM((A_GROUP, HEAD_DIM, tq), BF16)],
        compiler_params=_cparams(("parallel", "parallel", "arbitrary")),
        name="attn_a",
    )(proj3, proj3, proj3)


HB_HEADS = 4
HB_W = HB_HEADS * HEAD_DIM
SUB = 128
WIN = SUB + 2 * HALF_WIN
LSE_REP = LANES // HB_HEADS


def _attn_b_kernel(*refs, merged, seq_len):
    it = iter(refs)
    q_ref, kl_ref, km_ref, kr_ref, vl_ref, vm_ref, vr_ref, bb_ref = (next(it) for _ in range(8))
    parts = [(next(it), next(it)) for _ in merged]
    o_ref = next(it)
    l_ref = None if merged else next(it)
    kbuf, vbuf = next(it), next(it)
    flat = [(next(it), next(it)) for _ in merged]

    tq = q_ref.shape[0]
    n_sub = tq // SUB
    i = pl.program_id(3)
    kbuf[0:HALF_WIN, :] = kl_ref[...]
    kbuf[HALF_WIN:HALF_WIN + tq, :] = km_ref[...]
    kbuf[HALF_WIN + tq:, :] = kr_ref[...]
    ones = jnp.ones((tq + 2 * HALF_WIN, HEAD_DIM), BF16)
    for h in range(HB_HEADS):
        c0, w0 = h * HEAD_DIM, 2 * h * HEAD_DIM
        vbuf[0:HALF_WIN, w0:w0 + HEAD_DIM] = vl_ref[:, c0:c0 + HEAD_DIM]
        vbuf[HALF_WIN:HALF_WIN + tq, w0:w0 + HEAD_DIM] = vm_ref[:, c0:c0 + HEAD_DIM]
        vbuf[HALF_WIN + tq:, w0:w0 + HEAD_DIM] = vr_ref[:, c0:c0 + HEAD_DIM]
        vbuf[:, w0 + HEAD_DIM:w0 + 2 * HEAD_DIM] = ones

    for d, (po_ref, pl_ref), (fo_ref, fl_ref) in zip(merged, parts, flat):
        for r in range(d):
            fl_ref[pl.ds(r, tq // d, stride=d), :] = pl_ref[r]
            for h in range(HB_HEADS):
                fo_ref[h, pl.ds(r, tq // d, stride=d), :] = (
                    po_ref[r, :, h * HEAD_DIM:(h + 1) * HEAD_DIM].astype(F32))

    col = lax.broadcasted_iota(jnp.int32, (1, WIN), 1)
    lane_head = lax.broadcasted_iota(jnp.int32, (SUB, LANES), 1) // LSE_REP
    for sub in range(n_sub):
        r0 = sub * SUB
        kidx = i * tq + (r0 - HALF_WIN) + col
        colmask = jnp.where((kidx >= 0) & (kidx < seq_len), 0.0, NEG_INF).astype(F32)
        stats = jnp.zeros((SUB, LANES), F32)
        for h in range(HB_HEADS):
            c0 = h * HEAD_DIM
            q = q_ref[r0:r0 + SUB, c0:c0 + HEAD_DIM]
            k = kbuf[r0:r0 + WIN, c0:c0 + HEAD_DIM]
            s = lax.dot_general(q, k, (((1,), (1,)), ((), ())), preferred_element_type=F32)
            s = s + bb_ref[h]
            if sub == 0 or sub == n_sub - 1:
                s = s + colmask
            m = jnp.max(s, axis=-1, keepdims=True)
            p = jnp.exp2(s - m).astype(BF16)
            pv = jnp.dot(p, vbuf[r0:r0 + WIN, 2 * c0:2 * c0 + 2 * HEAD_DIM], preferred_element_type=F32)
            l = pv[:, HEAD_DIM:]
            o = pv[:, 0:HEAD_DIM] / l
            lse = m + jnp.log2(l)
            if merged:
                lses = [lse] + [fl_ref[r0:r0 + SUB, h * LSE_REP:h * LSE_REP + 1] for _, fl_ref in flat]
                outs = [o] + [fo_ref[h, r0:r0 + SUB, :] for fo_ref, _ in flat]
                top = functools.reduce(jnp.maximum, lses)
                ws = [jnp.exp2(x - top) for x in lses]
                o = sum(w * x for w, x in zip(ws, outs)) / sum(ws)
            else:
                stats = jnp.where(lane_head == h, lse, stats)
            o_ref[r0:r0 + SUB, c0:c0 + HEAD_DIM] = o.astype(o_ref.dtype)
        if not merged:
            l_ref[r0:r0 + SUB, :] = stats


def _band_bias(d):
    a = np.arange(SUB)[:, None]
    c = np.arange(WIN)[None, :]
    off = c - HALF_WIN - a
    slopes = (2.0 ** -np.arange(1, B_HEADS + 1)).astype(np.float32)
    bias = -slopes[:, None, None] * (np.abs(off) * d).astype(np.float32)[None] * np.float32(LOG2E)
    return np.where((np.abs(off) <= HALF_WIN)[None], bias, np.float32(NEG_INF)).astype(np.float32)


def _attn_b_pattern(src, cols, d, partials):
    b, _, seq_len, _ = src.shape
    col_q, col_k, col_v = cols
    tq = min(512, seq_len)
    nblk = seq_len // tq
    halo_per_blk = tq // HALF_WIN
    n_halo = seq_len // HALF_WIN
    n_hb = B_HEADS // HB_HEADS
    merged = tuple(dd for dd, _, _ in partials)
    bb = jnp.asarray(_band_bias(d))

    def main(col0, width=HB_W):
        return pl.BlockSpec((None, None, tq, width),
                            lambda bi, r, hb, i: (bi, r, i, col0 // width + hb))

    def left(col0):
        return pl.BlockSpec((None, None, HALF_WIN, HB_W),
                            lambda bi, r, hb, i: (bi, r, jnp.maximum(i * halo_per_blk - 1, 0),
                                                  col0 // HB_W + hb))

    def right(col0):
        return pl.BlockSpec((None, None, HALF_WIN, HB_W),
                            lambda bi, r, hb, i: (bi, r, jnp.minimum((i + 1) * halo_per_blk, n_halo - 1),
                                                  col0 // HB_W + hb))

    def part(dd, width):
        return pl.BlockSpec((None, dd, tq // dd, width), lambda bi, r, hb, i: (bi, 0, i, hb))

    in_specs = [main(col_q), left(col_k), main(col_k), right(col_k),
                left(col_v), main(col_v), right(col_v),
                pl.BlockSpec((HB_HEADS, SUB, WIN), lambda bi, r, hb, i: (hb, 0, 0))]
    args = [src] * 7 + [bb]
    scratch = [pltpu.VMEM((tq + 2 * HALF_WIN, HB_W), BF16), pltpu.VMEM((tq + 2 * HALF_WIN, 2 * HB_W), BF16)]
    for dd, o_part, l_part in partials:
        in_specs += [part(dd, HB_W), part(dd, LANES)]
        args += [o_part, l_part]
        scratch += [pltpu.VMEM((HB_HEADS, tq, LANES), F32), pltpu.VMEM((tq, LANES), F32)]
    if merged:
        out_shape = jax.ShapeDtypeStruct((b, d, seq_len, D_B), BF16)
        out_specs = main(0)
    else:
        out_shape = (jax.ShapeDtypeStruct((b, d, seq_len, D_B), BF16),
                     jax.ShapeDtypeStruct((b, d, seq_len, n_hb * LANES), F32))
        out_specs = (main(0), main(0, LANES))
    return pl.pallas_call(
        functools.partial(_attn_b_kernel, merged=merged, seq_len=seq_len),
        out_shape=out_shape,
        grid=(b, d, n_hb, nblk),
        in_specs=in_specs,
        out_specs=out_specs,
        scratch_shapes=scratch,
        compiler_params=_cparams(("parallel", "parallel", "parallel", "arbitrary")),
        name=f"attn_b_d{d}",
    )(*args)


def _attn_b(proj3, strided):
    b, t, _ = proj3.shape
    assert all(w // (2 * d) == HALF_WIN for w, d in DILATED_PATTERNS) and DILATED_PATTERNS[0][1] == 1
    partials = []
    for d, src in zip(STRIDES, strided):
        o_part, l_part = _attn_b_pattern(src, (0, D_B, 2 * D_B), d, [])
        partials.append((d, o_part, l_part))
    y = _attn_b_pattern(proj3.reshape(b, 1, t, D_IN), (COL_QB, COL_KB, COL_VB), 1, partials)
    return y.reshape(b, t, D_B)


def _gated(y_ref, g_refs, gain_ref):
    y = y_ref[...].astype(F32)
    ms = jnp.mean(y * y, axis=-1, keepdims=True)
    yn = y * lax.rsqrt(ms + EPS) * gain_ref[...]
    g = jnp.concatenate([r[...] for r in g_refs], axis=-1).astype(F32)
    return (yn * (g * jax.nn.sigmoid(g))).astype(BF16)


def _outproj_kernel(x_ref, ya_ref, yb_ref, ga0_ref, ga1_ref, gb0_ref, gb1_ref, na_ref, nb_ref,
                    w_ref, fn_ref, o_ref, *h_ref):
    za = _gated(ya_ref, (ga0_ref, ga1_ref), na_ref)
    zb = _gated(yb_ref, (gb0_ref, gb1_ref), nb_ref)
    out = x_ref[...]
    out = out + jnp.dot(za, w_ref[0:D_A, :], preferred_element_type=F32)
    out = out + jnp.dot(zb, w_ref[D_A:, :], preferred_element_type=F32)
    ms = jnp.mean(out * out, axis=-1, keepdims=True)
    normed = out * lax.rsqrt(ms + EPS) * fn_ref[...]
    if h_ref:
        o_ref[...] = out
        h_ref[0][...] = normed.astype(h_ref[0].dtype)
    else:
        o_ref[...] = normed


def _outproj(x2, ya2, yb2, proj2, na, nb, w, fn, *, layer, final, tm=512):
    m = x2.shape[0]
    half = D_A // 2

    def gate(col0, k):
        return pl.BlockSpec((tm, half), lambda i: (i, col0 // half + k))

    row = lambda width: pl.BlockSpec((tm, width), lambda i: (i, 0))
    const = lambda shape: pl.BlockSpec(shape, lambda i: (0, 0))
    stream = jax.ShapeDtypeStruct((m, D_MODEL), F32)
    return pl.pallas_call(
        _outproj_kernel,
        out_shape=stream if final else (stream, jax.ShapeDtypeStruct((m, D_MODEL), BF16)),
        grid=(m // tm,),
        in_specs=[row(D_MODEL), row(D_A), row(D_B),
                  gate(COL_GA, 0), gate(COL_GA, 1), gate(COL_GB, 0), gate(COL_GB, 1),
                  const((1, D_A)), const((1, D_B)),
                  pl.BlockSpec((None, D_A + D_B, D_MODEL), lambda i: (layer, 0, 0)), const((1, D_MODEL))],
        out_specs=row(D_MODEL) if final else (row(D_MODEL), row(D_MODEL)),
        compiler_params=_cparams(("parallel",)),
        name="outproj",
    )(x2, ya2, yb2, proj2, proj2, proj2, proj2, na, nb, w, fn)


def _rope_tables(t):
    pos = jnp.arange(t, dtype=jnp.int32)
    row = (pos // GRID_W).astype(F32)
    col = (pos % GRID_W).astype(F32)
    inv_freq = ROPE_THETA ** (-jnp.arange(0, AXIS_DIM, 2, dtype=F32) / AXIS_DIM)
    ang_r = row[:, None] * inv_freq[None, :]
    ang_c = col[:, None] * inv_freq[None, :]
    cos = jnp.concatenate([jnp.cos(ang_r)] * 2 + [jnp.cos(ang_c)] * 2, axis=-1)
    sin = jnp.concatenate([-jnp.sin(ang_r), jnp.sin(ang_r), -jnp.sin(ang_c), jnp.sin(ang_c)], axis=-1)
    return cos, sin


def kernel(x, norm_w, w_in, q_norm_a, k_norm_a, out_norm_a, out_norm_b, w_out, final_norm):
    b, t, d_model = x.shape
    depth = w_in.shape[0]
    assert d_model == D_MODEL and w_in.shape[1:] == (D_MODEL, D_IN)
    assert t % (DILATED_PATTERNS[-1][1] * SUB) == 0
    cos, sin = _rope_tables(t)
    col_scale = jnp.ones((D_IN,), F32).at[COL_QB:COL_KB].set(SCALE * LOG2E)
    w_in_bf = (w_in * col_scale).astype(BF16)
    w_in_bf = w_in_bf.reshape(depth, D_MODEL, D_IN // INPROJ_TN, INPROJ_TN).transpose(0, 2, 1, 3)
    w_out_bf = w_out.astype(BF16)
    x2 = x.reshape(b * t, D_MODEL)
    h2 = _norm(x2, norm_w[0][None])
    for l in range(depth):
        final = l == depth - 1
        proj2, *strided = _inproj(h2, w_in_bf, (q_norm_a[l] * (SCALE * LOG2E))[None],
                                  k_norm_a[l][None], cos, sin, layer=l, seq=t)
        proj3 = proj2.reshape(b, t, D_IN)
        ya = _attn_a(proj3)
        yb = _attn_b(proj3, strided)
        out = _outproj(x2, ya.reshape(b * t, D_A), yb.reshape(b * t, D_B), proj2,
                       out_norm_a[l][None], out_norm_b[l][None], w_out_bf,
                       (final_norm if final else norm_w[l + 1])[None], layer=l, final=final)
        x2, h2 = (out, None) if final else out
    return x2.reshape(b, t, D_MODEL)
```

```python
import functools

import numpy as np
import jax
import jax.numpy as jnp
from jax import lax
from jax.experimental import pallas as pl
from jax.experimental.pallas import tpu as pltpu

D_MODEL = 2048
HEAD_DIM = 128
A_HEADS = 8
A_KV_HEADS = 2
A_GROUP = A_HEADS // A_KV_HEADS
B_HEADS = 8
D_A = A_HEADS * HEAD_DIM
D_B = B_HEADS * HEAD_DIM
KV_DIM = A_KV_HEADS * HEAD_DIM
D_IN = 2 * D_A + 2 * KV_DIM + 4 * D_B
GRID_W = 64
AXIS_DIM = HEAD_DIM // 2
ROPE_THETA = 10000.0
DILATED_PATTERNS = ((128, 1), (512, 4), (2048, 16))
HALF_WIN = 64
SCALE = HEAD_DIM ** -0.5
LOG2E = float(np.log2(np.e))
EPS = 1e-6
NEG_INF = -1e30

COL_QA, COL_KA, COL_VA, COL_GA = 0, D_A, D_A + KV_DIM, D_A + 2 * KV_DIM
COL_QB = COL_GA + D_A
COL_KB, COL_VB, COL_GB = COL_QB + D_B, COL_QB + 2 * D_B, COL_QB + 3 * D_B

LANES = 128
VMEM_LIMIT = 56 * 1024 * 1024

BF16 = jnp.bfloat16
F32 = jnp.float32


def _cparams(sem):
    return pltpu.CompilerParams(dimension_semantics=sem, vmem_limit_bytes=VMEM_LIMIT)


def _rope_group(a, gain, cos, sin):
    ms = jnp.mean(a * a, axis=-1, keepdims=True)
    y = a * lax.rsqrt(ms + EPS) * gain
    lane = lax.broadcasted_iota(jnp.int32, y.shape, 1)
    partner = jnp.where((lane % AXIS_DIM) < AXIS_DIM // 2,
                        pltpu.roll(y, LANES - AXIS_DIM // 2, 1), pltpu.roll(y, AXIS_DIM // 2, 1))
    return y * cos + partner * sin


STRIDES = tuple(d for _, d in DILATED_PATTERNS if d > 1)
QKVB_W = COL_GB - COL_QB
ROW_CHUNK = 256
INPROJ_TN = 512
N_SLABS = 2


def _inproj_kernel(h_ref, w_ref, qg_ref, kg_ref, cos_ref, sin_ref, o_ref, *rest, tn):
    strided_refs, (slab_ref, slab2_ref) = rest[:len(STRIDES)], rest[len(STRIDES):]
    j = pl.program_id(1)
    tm = h_ref.shape[0]
    n_chunks = tm // ROW_CHUNK
    groups_per_tile = tn // HEAD_DIM
    n_q, n_k = D_A // HEAD_DIM, KV_DIM // HEAD_DIM
    n_special = -(-(n_q + n_k) // groups_per_tile)
    first_b, end_b = COL_QB // tn, COL_GB // tn

    def tile_body(kinds, strided):
        for ch in range(n_chunks):
            rows = slice(ch * ROW_CHUNK, (ch + 1) * ROW_CHUNK)
            acc = jnp.dot(h_ref[rows, :], w_ref[...], preferred_element_type=F32)
            for c, kind in enumerate(kinds):
                a = acc[:, c * HEAD_DIM:(c + 1) * HEAD_DIM]
                if kind is not None:
                    gain = qg_ref[...] if kind == 'q' else kg_ref[...]
                    a = _rope_group(a, gain, cos_ref[rows, :], sin_ref[rows, :])
                o_ref[rows, c * HEAD_DIM:(c + 1) * HEAD_DIM] = a.astype(o_ref.dtype)
                if strided:
                    slab_ref[ch % N_SLABS, c] = a
            if strided:
                (d1, ref1), (d2, ref2) = zip(STRIDES, strided_refs)
                n1, n2, step = ROW_CHUNK // d1, ROW_CHUNK // d2, d2 // d1
                for c in range(groups_per_tile):
                    cols = slice(c * LANES, (c + 1) * LANES)
                    for r1 in range(d1):
                        v = slab_ref[ch % N_SLABS, c, pl.ds(r1, n1, stride=d1), :]
                        ref1[r1, ch * n1:(ch + 1) * n1, cols] = v.astype(ref1.dtype)
                        slab2_ref[ch % N_SLABS, c, r1] = v
                    for r1 in range(d1):
                        for q2 in range(step):
                            v = slab2_ref[ch % N_SLABS, c, r1, pl.ds(q2, n2, stride=step), :]
                            ref2[q2 * d1 + r1, ch * n2:(ch + 1) * n2, cols] = v.astype(ref2.dtype)

    def kinds_of(tile):
        gs = [tile * groups_per_tile + c for c in range(groups_per_tile)]
        return tuple('q' if g < n_q else 'k' if g < n_q + n_k else None for g in gs)

    special = {}
    for tile in range(n_special):
        special.setdefault(kinds_of(tile), []).append(tile)
    for kinds, tiles in special.items():
        pl.when((j >= tiles[0]) & (j <= tiles[-1]))(functools.partial(tile_body, kinds, False))
    plain = (None,) * groups_per_tile
    pl.when((j >= first_b) & (j < end_b))(functools.partial(tile_body, plain, True))
    pl.when((j >= n_special) & ((j < first_b) | (j >= end_b)))(functools.partial(tile_body, plain, False))


def _norm_kernel(x_ref, nw_ref, h_ref):
    x = x_ref[...]
    ms = jnp.mean(x * x, axis=-1, keepdims=True)
    h_ref[...] = (x * lax.rsqrt(ms + EPS) * nw_ref[...]).astype(h_ref.dtype)


def _norm(x2, nw, *, tm=1024):
    m = x2.shape[0]
    return pl.pallas_call(
        _norm_kernel,
        out_shape=jax.ShapeDtypeStruct((m, D_MODEL), BF16),
        grid=(m // tm,),
        in_specs=[pl.BlockSpec((tm, D_MODEL), lambda i: (i, 0)), pl.BlockSpec((1, D_MODEL), lambda i: (0, 0))],
        out_specs=pl.BlockSpec((tm, D_MODEL), lambda i: (i, 0)),
        compiler_params=_cparams(("parallel",)),
        name="norm",
    )(x2, nw)


def _inproj(h2, w, qg, kg, cos, sin, *, layer, seq, tm=2048, tn=INPROJ_TN):
    m = h2.shape[0]
    tm = min(tm, seq)
    t_blocks = seq // tm
    batch = m // seq
    assert COL_QB % tn == 0 and COL_GB % tn == 0
    assert len(STRIDES) == 2 and STRIDES[1] % STRIDES[0] == 0 and ROW_CHUNK % STRIDES[1] == 0
    first_b, n_b = COL_QB // tn, QKVB_W // tn

    def strided_spec(d):
        return pl.BlockSpec((None, d, tm // d, tn),
                            lambda i, j: (i // t_blocks, 0, i % t_blocks, jnp.clip(j - first_b, 0, n_b - 1)))

    return pl.pallas_call(
        functools.partial(_inproj_kernel, tn=tn),
        out_shape=(jax.ShapeDtypeStruct((m, D_IN), BF16),)
        + tuple(jax.ShapeDtypeStruct((batch, d, seq // d, QKVB_W), BF16) for d in STRIDES),
        grid=(m // tm, D_IN // tn),
        in_specs=[
            pl.BlockSpec((tm, D_MODEL), lambda i, j: (i, 0)),
            pl.BlockSpec((None, D_MODEL, tn), lambda i, j: (layer, 0, j)),
            pl.BlockSpec((1, HEAD_DIM), lambda i, j: (0, 0)),
            pl.BlockSpec((1, HEAD_DIM), lambda i, j: (0, 0)),
            pl.BlockSpec((tm, HEAD_DIM), lambda i, j: (i % t_blocks, 0)),
            pl.BlockSpec((tm, HEAD_DIM), lambda i, j: (i % t_blocks, 0)),
        ],
        out_specs=(pl.BlockSpec((tm, tn), lambda i, j: (i, j)),) + tuple(strided_spec(d) for d in STRIDES),
        scratch_shapes=[pltpu.VMEM((N_SLABS, tn // LANES, ROW_CHUNK, LANES), F32),
                        pltpu.VMEM((N_SLABS, tn // LANES, STRIDES[0], ROW_CHUNK // STRIDES[0], LANES), F32)],
        compiler_params=_cparams(("arbitrary", "arbitrary")),
        name="inproj",
    )(h2, w, qg, kg, cos, sin)


ONES_ROWS = 16
KV_UNROLL = 4


def _attn_a_kernel(q_ref, k_ref, v_ref, o_ref, vt_ref, s0_ref, s1_ref, acc_ref, qt_ref, *, tk):
    tq = q_ref.shape[0]
    n_kv = k_ref.shape[0] // tk

    @pl.when(pl.program_id(2) == 0)
    def _():
        for kk in range(n_kv):
            vt_ref[kk, 0:HEAD_DIM, :] = v_ref[kk * tk:(kk + 1) * tk, :].astype(F32).T.astype(BF16)
            vt_ref[kk, HEAD_DIM:, :] = jnp.ones((ONES_ROWS, tk), BF16)

    sbuf = (s0_ref, s1_ref)
    for hh in range(A_GROUP):
        qt_ref[hh] = q_ref[:, hh * HEAD_DIM:(hh + 1) * HEAD_DIM].astype(F32).T.astype(BF16)

    def scores(kk, dst):
        start = pl.multiple_of(kk * tk, tk)
        k = k_ref[pl.ds(start, tk), :]
        for hh in range(A_GROUP):
            dst[hh] = jnp.dot(k, qt_ref[hh], preferred_element_type=F32)

    def softmax_pv(kk, src, ml):
        vt = vt_ref[kk]
        new = []
        for hh in range(A_GROUP):
            m, l = ml[hh]
            s = src[hh]
            m_new = jnp.maximum(m, jnp.max(s, axis=0, keepdims=True))
            alpha = jnp.exp2(m - m_new)
            p = jnp.exp2(s - m_new).astype(BF16)
            pv = jnp.dot(vt, p, preferred_element_type=F32)
            acc_ref[hh] = alpha * acc_ref[hh] + pv[0:HEAD_DIM]
            new.append((m_new, alpha * l + pv[HEAD_DIM:HEAD_DIM + 1]))
        return tuple(new)

    def run(kk0, ml, n_tiles, n_scores):
        for u in range(n_tiles):
            if u < n_scores:
                scores(kk0 + u + 1, sbuf[(u + 1) % 2])
            ml = softmax_pv(kk0 + u, sbuf[u % 2], ml)
        return ml

    acc_ref[...] = jnp.zeros_like(acc_ref)
    scores(0, sbuf[0])
    ml = tuple((jnp.full((1, tq), -jnp.inf, F32), jnp.zeros((1, tq), F32)) for _ in range(A_GROUP))
    unroll = min(KV_UNROLL, n_kv)
    ml = lax.fori_loop(0, n_kv // unroll - 1, lambda jj, ml: run(jj * unroll, ml, unroll, unroll), ml)
    ml = run(n_kv - unroll, ml, unroll, unroll - 1)
    for hh in range(A_GROUP):
        o_ref[:, hh * HEAD_DIM:(hh + 1) * HEAD_DIM] = (acc_ref[hh] / ml[hh][1]).T.astype(o_ref.dtype)


def _attn_a(proj3, *, tq=256, tk=512):
    b, t, _ = proj3.shape
    gw = A_GROUP * HEAD_DIM
    return pl.pallas_call(
        functools.partial(_attn_a_kernel, tk=tk),
        out_shape=jax.ShapeDtypeStruct((b, t, D_A), BF16),
        grid=(b, A_KV_HEADS, t // tq),
        in_specs=[
            pl.BlockSpec((None, tq, gw), lambda bi, g, qi: (bi, qi, COL_QA // gw + g)),
            pl.BlockSpec((None, t, HEAD_DIM), lambda bi, g, qi: (bi, 0, COL_KA // HEAD_DIM + g)),
            pl.BlockSpec((None, t, HEAD_DIM), lambda bi, g, qi: (bi, 0, COL_VA // HEAD_DIM + g)),
        ],
        out_specs=pl.BlockSpec((None, tq, gw), lambda bi, g, qi: (bi, qi, g)),
        scratch_shapes=[pltpu.VMEM((t // tk, HEAD_DIM + ONES_ROWS, tk), BF16),
                        pltpu.VMEM((A_GROUP, tk, tq), F32), pltpu.VMEM((A_GROUP, tk, tq), F32),
                        pltpu.VMEM((A_GROUP, HEAD_DIM, tq), F32), pltpu.VMEM((A_GROUP, HEAD_DIM, tq), BF16)],
        compiler_params=_cparams(("parallel", "parallel", "arbitrary")),
        name="attn_a",
    )(proj3, proj3, proj3)


SUB = 128
WIN = SUB + 2 * HALF_WIN
STAT_HEADS = 4
LSE_REP = LANES // STAT_HEADS


def _attn_b_kernel(*refs, merged, seq_len):
    it = iter(refs)
    q_ref, kl_ref, km_ref, kr_ref, vl_ref, vm_ref, vr_ref, bb_ref = (next(it) for _ in range(8))
    parts = [(next(it), next(it)) for _ in merged]
    o_ref = next(it)
    l_ref = None if merged else next(it)
    kbuf, vbuf = next(it), next(it)
    flat = [(next(it), next(it)) for _ in merged]

    tq = q_ref.shape[0]
    n_heads = q_ref.shape[1] // HEAD_DIM
    n_sub = tq // SUB
    i = pl.program_id(3)
    kbuf[0:HALF_WIN, :] = kl_ref[...]
    kbuf[HALF_WIN:HALF_WIN + tq, :] = km_ref[...]
    kbuf[HALF_WIN + tq:, :] = kr_ref[...]
    ones = jnp.ones((tq + 2 * HALF_WIN, HEAD_DIM), BF16)
    for h in range(n_heads):
        c0, w0 = h * HEAD_DIM, 2 * h * HEAD_DIM
        vbuf[0:HALF_WIN, w0:w0 + HEAD_DIM] = vl_ref[:, c0:c0 + HEAD_DIM]
        vbuf[HALF_WIN:HALF_WIN + tq, w0:w0 + HEAD_DIM] = vm_ref[:, c0:c0 + HEAD_DIM]
        vbuf[HALF_WIN + tq:, w0:w0 + HEAD_DIM] = vr_ref[:, c0:c0 + HEAD_DIM]
        vbuf[:, w0 + HEAD_DIM:w0 + 2 * HEAD_DIM] = ones

    for d, (po_ref, pl_ref), (fo_ref, fl_ref) in zip(merged, parts, flat):
        for r in range(d):
            fl_ref[pl.ds(r, tq // d, stride=d), :] = pl_ref[r]
            for h in range(n_heads):
                fo_ref[h, pl.ds(r, tq // d, stride=d), :] = (
                    po_ref[r, :, h * HEAD_DIM:(h + 1) * HEAD_DIM].astype(F32))

    col = lax.broadcasted_iota(jnp.int32, (1, WIN), 1)
    lane_head = lax.broadcasted_iota(jnp.int32, (SUB, LANES), 1) // LSE_REP
    for sub in range(n_sub):
        r0 = sub * SUB
        kidx = i * tq + (r0 - HALF_WIN) + col
        colmask = jnp.where((kidx >= 0) & (kidx < seq_len), 0.0, NEG_INF).astype(F32)
        stats = [jnp.zeros((SUB, LANES), F32)] * (n_heads // STAT_HEADS)
        for h in range(n_heads):
            c0 = h * HEAD_DIM
            q = q_ref[r0:r0 + SUB, c0:c0 + HEAD_DIM]
            k = kbuf[r0:r0 + WIN, c0:c0 + HEAD_DIM]
            s = lax.dot_general(q, k, (((1,), (1,)), ((), ())), preferred_element_type=F32)
            s = s + bb_ref[h]
            if sub == 0 or sub == n_sub - 1:
                s = s + colmask
            m = jnp.max(s, axis=-1, keepdims=True)
            p = jnp.exp2(s - m).astype(BF16)
            pv = jnp.dot(p, vbuf[r0:r0 + WIN, 2 * c0:2 * c0 + 2 * HEAD_DIM], preferred_element_type=F32)
            l = pv[:, HEAD_DIM:]
            o = pv[:, 0:HEAD_DIM] / l
            lse = m + jnp.log2(l)
            if merged:
                lses = [lse] + [fl_ref[r0:r0 + SUB, h * LSE_REP:h * LSE_REP + 1] for _, fl_ref in flat]
                outs = [o] + [fo_ref[h, r0:r0 + SUB, :] for fo_ref, _ in flat]
                top = functools.reduce(jnp.maximum, lses)
                ws = [jnp.exp2(x - top) for x in lses]
                o = sum(w * x for w, x in zip(ws, outs)) / sum(ws)
            else:
                st = h // STAT_HEADS
                stats[st] = jnp.where(lane_head == h % STAT_HEADS, lse, stats[st])
            o_ref[r0:r0 + SUB, c0:c0 + HEAD_DIM] = o.astype(o_ref.dtype)
        if not merged:
            for st, tile in enumerate(stats):
                l_ref[r0:r0 + SUB, st * LANES:(st + 1) * LANES] = tile


def _band_bias(d):
    a = np.arange(SUB)[:, None]
    c = np.arange(WIN)[None, :]
    off = c - HALF_WIN - a
    slopes = (2.0 ** -np.arange(1, B_HEADS + 1)).astype(np.float32)
    bias = -slopes[:, None, None] * (np.abs(off) * d).astype(np.float32)[None] * np.float32(LOG2E)
    return np.where((np.abs(off) <= HALF_WIN)[None], bias, np.float32(NEG_INF)).astype(np.float32)


def _attn_b_pattern(src, cols, d, partials, *, hb_heads, tq):
    b, _, seq_len, _ = src.shape
    col_q, col_k, col_v = cols
    tq = min(tq, seq_len)
    hb_w = hb_heads * HEAD_DIM
    stat_w = hb_heads // STAT_HEADS * LANES
    assert all(c % hb_w == 0 for c in cols) and hb_heads % STAT_HEADS == 0
    nblk = seq_len // tq
    halo_per_blk = tq // HALF_WIN
    n_halo = seq_len // HALF_WIN
    n_hb = B_HEADS // hb_heads
    merged = tuple(dd for dd, _, _ in partials)
    assert not merged or hb_heads == STAT_HEADS
    bb = jnp.asarray(_band_bias(d))

    def main(col0, width=hb_w):
        return pl.BlockSpec((None, None, tq, width),
                            lambda bi, r, hb, i: (bi, r, i, col0 // width + hb))

    def left(col0):
        return pl.BlockSpec((None, None, HALF_WIN, hb_w),
                            lambda bi, r, hb, i: (bi, r, jnp.maximum(i * halo_per_blk - 1, 0),
                                                  col0 // hb_w + hb))

    def right(col0):
        return pl.BlockSpec((None, None, HALF_WIN, hb_w),
                            lambda bi, r, hb, i: (bi, r, jnp.minimum((i + 1) * halo_per_blk, n_halo - 1),
                                                  col0 // hb_w + hb))

    def part(dd, width):
        return pl.BlockSpec((None, dd, tq // dd, width), lambda bi, r, hb, i: (bi, 0, i, hb))

    in_specs = [main(col_q), left(col_k), main(col_k), right(col_k),
                left(col_v), main(col_v), right(col_v),
                pl.BlockSpec((hb_heads, SUB, WIN), lambda bi, r, hb, i: (hb, 0, 0))]
    args = [src] * 7 + [bb]
    scratch = [pltpu.VMEM((tq + 2 * HALF_WIN, hb_w), BF16), pltpu.VMEM((tq + 2 * HALF_WIN, 2 * hb_w), BF16)]
    for dd, o_part, l_part in partials:
        in_specs += [part(dd, hb_w), part(dd, LANES)]
        args += [o_part, l_part]
        scratch += [pltpu.VMEM((hb_heads, tq, LANES), F32), pltpu.VMEM((tq, LANES), F32)]
    if merged:
        out_shape = jax.ShapeDtypeStruct((b, d, seq_len, D_B), BF16)
        out_specs = main(0)
    else:
        out_shape = (jax.ShapeDtypeStruct((b, d, seq_len, D_B), BF16),
                     jax.ShapeDtypeStruct((b, d, seq_len, B_HEADS // STAT_HEADS * LANES), F32))
        out_specs = (main(0), main(0, stat_w))
    return pl.pallas_call(
        functools.partial(_attn_b_kernel, merged=merged, seq_len=seq_len),
        out_shape=out_shape,
        grid=(b, d, n_hb, nblk),
        in_specs=in_specs,
        out_specs=out_specs,
        scratch_shapes=scratch,
        compiler_params=_cparams(("parallel", "parallel", "parallel", "arbitrary")),
        name=f"attn_b_d{d}",
    )(*args)


def _attn_b(proj3, strided):
    b, t, _ = proj3.shape
    assert all(w // (2 * d) == HALF_WIN for w, d in DILATED_PATTERNS) and DILATED_PATTERNS[0][1] == 1
    partials = []
    for d, src in zip(STRIDES, strided):
        o_part, l_part = _attn_b_pattern(src, (0, D_B, 2 * D_B), d, [], hb_heads=B_HEADS, tq=1024)
        partials.append((d, o_part, l_part))
    y = _attn_b_pattern(proj3.reshape(b, 1, t, D_IN), (COL_QB, COL_KB, COL_VB), 1, partials,
                        hb_heads=STAT_HEADS, tq=1024)
    return y.reshape(b, t, D_B)


OUT_CHUNK = 512


def _gated(y_ref, g_refs, gain_ref, rows):
    y = y_ref[rows, :].astype(F32)
    ms = jnp.mean(y * y, axis=-1, keepdims=True)
    yn = y * lax.rsqrt(ms + EPS) * gain_ref[...]
    g = jnp.concatenate([r[rows, :] for r in g_refs], axis=-1).astype(F32)
    return (yn * (g * jax.nn.sigmoid(g))).astype(BF16)


def _outproj_kernel(x_ref, ya_ref, yb_ref, ga0_ref, ga1_ref, gb0_ref, gb1_ref, na_ref, nb_ref,
                    w_ref, fn_ref, o_ref, *h_ref):
    for ch in range(x_ref.shape[0] // OUT_CHUNK):
        rows = slice(ch * OUT_CHUNK, (ch + 1) * OUT_CHUNK)
        za = _gated(ya_ref, (ga0_ref, ga1_ref), na_ref, rows)
        zb = _gated(yb_ref, (gb0_ref, gb1_ref), nb_ref, rows)
        out = x_ref[rows, :]
        out = out + jnp.dot(za, w_ref[0:D_A, :], preferred_element_type=F32)
        out = out + jnp.dot(zb, w_ref[D_A:, :], preferred_element_type=F32)
        ms = jnp.mean(out * out, axis=-1, keepdims=True)
        normed = out * lax.rsqrt(ms + EPS) * fn_ref[...]
        if h_ref:
            o_ref[rows, :] = out
            h_ref[0][rows, :] = normed.astype(h_ref[0].dtype)
        else:
            o_ref[rows, :] = normed


def _outproj(x2, ya2, yb2, proj2, na, nb, w, fn, *, layer, final, tm=512):
    m = x2.shape[0]
    half = D_A // 2

    def gate(col0, k):
        return pl.BlockSpec((tm, half), lambda i: (i, col0 // half + k))

    row = lambda width: pl.BlockSpec((tm, width), lambda i: (i, 0))
    const = lambda shape: pl.BlockSpec(shape, lambda i: (0, 0))
    stream = jax.ShapeDtypeStruct((m, D_MODEL), F32)
    return pl.pallas_call(
        _outproj_kernel,
        out_shape=stream if final else (stream, jax.ShapeDtypeStruct((m, D_MODEL), BF16)),
        grid=(m // tm,),
        in_specs=[row(D_MODEL), row(D_A), row(D_B),
                  gate(COL_GA, 0), gate(COL_GA, 1), gate(COL_GB, 0), gate(COL_GB, 1),
                  const((1, D_A)), const((1, D_B)),
                  pl.BlockSpec((None, D_A + D_B, D_MODEL), lambda i: (layer, 0, 0)), const((1, D_MODEL))],
        out_specs=row(D_MODEL) if final else (row(D_MODEL), row(D_MODEL)),
        compiler_params=_cparams(("parallel",)),
        name="outproj",
    )(x2, ya2, yb2, proj2, proj2, proj2, proj2, na, nb, w, fn)


def _rope_tables(t):
    pos = jnp.arange(t, dtype=jnp.int32)
    row = (pos // GRID_W).astype(F32)
    col = (pos % GRID_W).astype(F32)
    inv_freq = ROPE_THETA ** (-jnp.arange(0, AXIS_DIM, 2, dtype=F32) / AXIS_DIM)
    ang_r = row[:, None] * inv_freq[None, :]
    ang_c = col[:, None] * inv_freq[None, :]
    cos = jnp.concatenate([jnp.cos(ang_r)] * 2 + [jnp.cos(ang_c)] * 2, axis=-1)
    sin = jnp.concatenate([-jnp.sin(ang_r), jnp.sin(ang_r), -jnp.sin(ang_c), jnp.sin(ang_c)], axis=-1)
    return cos, sin


def kernel(x, norm_w, w_in, q_norm_a, k_norm_a, out_norm_a, out_norm_b, w_out, final_norm):
    b, t, d_model = x.shape
    depth = w_in.shape[0]
    assert d_model == D_MODEL and w_in.shape[1:] == (D_MODEL, D_IN)
    assert t % (DILATED_PATTERNS[-1][1] * SUB) == 0
    cos, sin = _rope_tables(t)
    col_scale = jnp.ones((D_IN,), F32).at[COL_QB:COL_KB].set(SCALE * LOG2E)
    w_in_bf = (w_in * col_scale).astype(BF16)
    w_out_bf = w_out.astype(BF16)
    x2 = x.reshape(b * t, D_MODEL)
    h2 = _norm(x2, norm_w[0][None])
    for l in range(depth):
        final = l == depth - 1
        proj2, *strided = _inproj(h2, w_in_bf, (q_norm_a[l] * (SCALE * LOG2E))[None],
                                  k_norm_a[l][None], cos, sin, layer=l, seq=t)
        proj3 = proj2.reshape(b, t, D_IN)
        ya = _attn_a(proj3)
        yb = _attn_b(proj3, strided)
        out = _outproj(x2, ya.reshape(b * t, D_A), yb.reshape(b * t, D_B), proj2,
                       out_norm_a[l][None], out_norm_b[l][None], w_out_bf,
                       (final_norm if final else norm_w[l + 1])[None], layer=l, final=final)
        x2, h2 = (out, None) if final else out
    return x2.reshape(b, t, D_MODEL)
```

```python
import functools

import numpy as np
import jax
import jax.numpy as jnp
from jax import lax
from jax.experimental import pallas as pl
from jax.experimental.pallas import tpu as pltpu

D_MODEL = 2048
HEAD_DIM = 128
A_HEADS = 8
A_KV_HEADS = 2
A_GROUP = A_HEADS // A_KV_HEADS
B_HEADS = 8
D_A = A_HEADS * HEAD_DIM
D_B = B_HEADS * HEAD_DIM
KV_DIM = A_KV_HEADS * HEAD_DIM
D_IN = 2 * D_A + 2 * KV_DIM + 4 * D_B
GRID_W = 64
AXIS_DIM = HEAD_DIM // 2
ROPE_THETA = 10000.0
DILATED_PATTERNS = ((128, 1), (512, 4), (2048, 16))
HALF_WIN = 64
SCALE = HEAD_DIM ** -0.5
LOG2E = float(np.log2(np.e))
EPS = 1e-6
NEG_INF = -1e30

COL_QA, COL_KA, COL_VA, COL_GA = 0, D_A, D_A + KV_DIM, D_A + 2 * KV_DIM
COL_QB = COL_GA + D_A
COL_KB, COL_VB, COL_GB = COL_QB + D_B, COL_QB + 2 * D_B, COL_QB + 3 * D_B

LANES = 128
VMEM_LIMIT = 56 * 1024 * 1024

BF16 = jnp.bfloat16
F32 = jnp.float32


def _cparams(sem):
    return pltpu.CompilerParams(dimension_semantics=sem, vmem_limit_bytes=VMEM_LIMIT)


def _rope_group(a, gain, cos, sin):
    ms = jnp.mean(a * a, axis=-1, keepdims=True)
    y = a * lax.rsqrt(ms + EPS) * gain
    lane = lax.broadcasted_iota(jnp.int32, y.shape, 1)
    partner = jnp.where((lane % AXIS_DIM) < AXIS_DIM // 2,
                        pltpu.roll(y, LANES - AXIS_DIM // 2, 1), pltpu.roll(y, AXIS_DIM // 2, 1))
    return y * cos + partner * sin


STRIDES = tuple(d for _, d in DILATED_PATTERNS if d > 1)
QKVB_W = COL_GB - COL_QB
ROW_CHUNK = 256
INPROJ_TN = 512
N_SLABS = 2


def _inproj_kernel(h_ref, w_ref, qg_ref, kg_ref, cos_ref, sin_ref, o_ref, *rest, tn):
    strided_refs, (slab_ref, slab2_ref) = rest[:len(STRIDES)], rest[len(STRIDES):]
    j = pl.program_id(1)
    tm = h_ref.shape[0]
    n_chunks = tm // ROW_CHUNK
    groups_per_tile = tn // HEAD_DIM
    n_q, n_k = D_A // HEAD_DIM, KV_DIM // HEAD_DIM
    n_special = -(-(n_q + n_k) // groups_per_tile)
    first_b, end_b = COL_QB // tn, COL_GB // tn

    def tile_body(kinds, strided):
        for ch in range(n_chunks):
            rows = slice(ch * ROW_CHUNK, (ch + 1) * ROW_CHUNK)
            acc = jnp.dot(h_ref[rows, :], w_ref[...], preferred_element_type=F32)
            for c, kind in enumerate(kinds):
                a = acc[:, c * HEAD_DIM:(c + 1) * HEAD_DIM]
                if kind is not None:
                    gain = qg_ref[...] if kind == 'q' else kg_ref[...]
                    a = _rope_group(a, gain, cos_ref[rows, :], sin_ref[rows, :])
                o_ref[rows, c * HEAD_DIM:(c + 1) * HEAD_DIM] = a.astype(o_ref.dtype)
                if strided:
                    slab_ref[ch % N_SLABS, c] = a
            if strided:
                (d1, ref1), (d2, ref2) = zip(STRIDES, strided_refs)
                n1, n2, step = ROW_CHUNK // d1, ROW_CHUNK // d2, d2 // d1
                for c in range(groups_per_tile):
                    cols = slice(c * LANES, (c + 1) * LANES)
                    for r1 in range(d1):
                        v = slab_ref[ch % N_SLABS, c, pl.ds(r1, n1, stride=d1), :]
                        ref1[r1, ch * n1:(ch + 1) * n1, cols] = v.astype(ref1.dtype)
                        slab2_ref[ch % N_SLABS, c, r1] = v
                    for r1 in range(d1):
                        for q2 in range(step):
                            v = slab2_ref[ch % N_SLABS, c, r1, pl.ds(q2, n2, stride=step), :]
                            ref2[q2 * d1 + r1, ch * n2:(ch + 1) * n2, cols] = v.astype(ref2.dtype)

    def kinds_of(tile):
        gs = [tile * groups_per_tile + c for c in range(groups_per_tile)]
        return tuple('q' if g < n_q else 'k' if g < n_q + n_k else None for g in gs)

    special = {}
    for tile in range(n_special):
        special.setdefault(kinds_of(tile), []).append(tile)
    for kinds, tiles in special.items():
        pl.when((j >= tiles[0]) & (j <= tiles[-1]))(functools.partial(tile_body, kinds, False))
    plain = (None,) * groups_per_tile
    pl.when((j >= first_b) & (j < end_b))(functools.partial(tile_body, plain, True))
    pl.when((j >= n_special) & ((j < first_b) | (j >= end_b)))(functools.partial(tile_body, plain, False))


def _norm_kernel(x_ref, nw_ref, h_ref):
    x = x_ref[...]
    ms = jnp.mean(x * x, axis=-1, keepdims=True)
    h_ref[...] = (x * lax.rsqrt(ms + EPS) * nw_ref[...]).astype(h_ref.dtype)


def _norm(x2, nw, *, tm=1024):
    m = x2.shape[0]
    return pl.pallas_call(
        _norm_kernel,
        out_shape=jax.ShapeDtypeStruct((m, D_MODEL), BF16),
        grid=(m // tm,),
        in_specs=[pl.BlockSpec((tm, D_MODEL), lambda i: (i, 0)), pl.BlockSpec((1, D_MODEL), lambda i: (0, 0))],
        out_specs=pl.BlockSpec((tm, D_MODEL), lambda i: (i, 0)),
        compiler_params=_cparams(("parallel",)),
        name="norm",
    )(x2, nw)


def _inproj(h2, w, qg, kg, cos, sin, *, layer, seq, tm=2048, tn=INPROJ_TN):
    m = h2.shape[0]
    tm = min(tm, seq)
    t_blocks = seq // tm
    batch = m // seq
    assert COL_QB % tn == 0 and COL_GB % tn == 0
    assert len(STRIDES) == 2 and STRIDES[1] % STRIDES[0] == 0 and ROW_CHUNK % STRIDES[1] == 0
    first_b, n_b = COL_QB // tn, QKVB_W // tn

    def strided_spec(d):
        return pl.BlockSpec((None, d, tm // d, tn),
                            lambda i, j: (i // t_blocks, 0, i % t_blocks, jnp.clip(j - first_b, 0, n_b - 1)))

    return pl.pallas_call(
        functools.partial(_inproj_kernel, tn=tn),
        out_shape=(jax.ShapeDtypeStruct((m, D_IN), BF16),)
        + tuple(jax.ShapeDtypeStruct((batch, d, seq // d, QKVB_W), BF16) for d in STRIDES),
        grid=(m // tm, D_IN // tn),
        in_specs=[
            pl.BlockSpec((tm, D_MODEL), lambda i, j: (i, 0)),
            pl.BlockSpec((None, D_MODEL, tn), lambda i, j: (layer, 0, j)),
            pl.BlockSpec((1, HEAD_DIM), lambda i, j: (0, 0)),
            pl.BlockSpec((1, HEAD_DIM), lambda i, j: (0, 0)),
            pl.BlockSpec((tm, HEAD_DIM), lambda i, j: (i % t_blocks, 0)),
            pl.BlockSpec((tm, HEAD_DIM), lambda i, j: (i % t_blocks, 0)),
        ],
        out_specs=(pl.BlockSpec((tm, tn), lambda i, j: (i, j)),) + tuple(strided_spec(d) for d in STRIDES),
        scratch_shapes=[pltpu.VMEM((N_SLABS, tn // LANES, ROW_CHUNK, LANES), F32),
                        pltpu.VMEM((N_SLABS, tn // LANES, STRIDES[0], ROW_CHUNK // STRIDES[0], LANES), F32)],
        compiler_params=_cparams(("arbitrary", "arbitrary")),
        name="inproj",
    )(h2, w, qg, kg, cos, sin)


ONES_ROWS = 16
KV_UNROLL = 4


def _attn_a_kernel(q_ref, k_ref, v_ref, o_ref, vt_ref, s0_ref, s1_ref, acc_ref, qt_ref, *, tk, tq):
    n_q = q_ref.shape[0] // tq
    n_kv = k_ref.shape[0] // tk
    assert n_kv % 2 == 0

    @pl.when(pl.program_id(2) == 0)
    def _():
        for kk in range(n_kv):
            vt_ref[kk, 0:HEAD_DIM, :] = v_ref[kk * tk:(kk + 1) * tk, :].astype(F32).T.astype(BF16)
            vt_ref[kk, HEAD_DIM:, :] = jnp.ones((ONES_ROWS, tk), BF16)

    sbuf = (s0_ref, s1_ref)
    for qi in range(n_q):
        for hh in range(A_GROUP):
            qt_ref[qi, hh] = (q_ref[qi * tq:(qi + 1) * tq, hh * HEAD_DIM:(hh + 1) * HEAD_DIM]
                              .astype(F32).T.astype(BF16))

    def scores(qi, kk, dst):
        start = pl.multiple_of(kk * tk, tk)
        k = k_ref[pl.ds(start, tk), :]
        for hh in range(A_GROUP):
            dst[hh] = jnp.dot(k, qt_ref[qi, hh], preferred_element_type=F32)

    def softmax_pv(kk, src, ml):
        vt = vt_ref[kk]
        new = []
        for hh in range(A_GROUP):
            m, l = ml[hh]
            s = src[hh]
            m_new = jnp.maximum(m, jnp.max(s, axis=0, keepdims=True))
            alpha = jnp.exp2(m - m_new)
            p = jnp.exp2(s - m_new).astype(BF16)
            pv = jnp.dot(vt, p, preferred_element_type=F32)
            acc_ref[hh] = alpha * acc_ref[hh] + pv[0:HEAD_DIM]
            new.append((m_new, alpha * l + pv[HEAD_DIM:HEAD_DIM + 1]))
        return tuple(new)

    unroll = min(KV_UNROLL, n_kv)

    def trip(qi, kk0, ml, last):
        for u in range(unroll):
            if last and u == unroll - 1:
                scores(jnp.minimum(qi + 1, n_q - 1), 0, sbuf[(u + 1) % 2])
            else:
                scores(qi, kk0 + u + 1, sbuf[(u + 1) % 2])
            ml = softmax_pv(kk0 + u, sbuf[u % 2], ml)
        return ml

    def query_tile(qi, carry):
        acc_ref[...] = jnp.zeros_like(acc_ref)
        ml = tuple((jnp.full((1, tq), -jnp.inf, F32), jnp.zeros((1, tq), F32)) for _ in range(A_GROUP))
        ml = lax.fori_loop(0, n_kv // unroll - 1, lambda jj, ml: trip(qi, jj * unroll, ml, False), ml)
        ml = trip(qi, n_kv - unroll, ml, True)
        rows = pl.ds(pl.multiple_of(qi * tq, tq), tq)
        for hh in range(A_GROUP):
            o_ref[rows, hh * HEAD_DIM:(hh + 1) * HEAD_DIM] = (acc_ref[hh] / ml[hh][1]).T.astype(o_ref.dtype)
        return carry

    scores(0, 0, sbuf[0])
    lax.fori_loop(0, n_q, query_tile, 0)


def _attn_a(proj3, *, tq=256, n_q=4, tk=512):
    b, t, _ = proj3.shape
    gw = A_GROUP * HEAD_DIM
    rows = n_q * tq
    return pl.pallas_call(
        functools.partial(_attn_a_kernel, tk=tk, tq=tq),
        out_shape=jax.ShapeDtypeStruct((b, t, D_A), BF16),
        grid=(b, A_KV_HEADS, t // rows),
        in_specs=[
            pl.BlockSpec((None, rows, gw), lambda bi, g, qi: (bi, qi, COL_QA // gw + g)),
            pl.BlockSpec((None, t, HEAD_DIM), lambda bi, g, qi: (bi, 0, COL_KA // HEAD_DIM + g)),
            pl.BlockSpec((None, t, HEAD_DIM), lambda bi, g, qi: (bi, 0, COL_VA // HEAD_DIM + g)),
        ],
        out_specs=pl.BlockSpec((None, rows, gw), lambda bi, g, qi: (bi, qi, g)),
        scratch_shapes=[pltpu.VMEM((t // tk, HEAD_DIM + ONES_ROWS, tk), BF16),
                        pltpu.VMEM((A_GROUP, tk, tq), F32), pltpu.VMEM((A_GROUP, tk, tq), F32),
                        pltpu.VMEM((A_GROUP, HEAD_DIM, tq), F32),
                        pltpu.VMEM((n_q, A_GROUP, HEAD_DIM, tq), BF16)],
        compiler_params=_cparams(("parallel", "parallel", "arbitrary")),
        name="attn_a",
    )(proj3, proj3, proj3)


SUB = 128
WIN = SUB + 2 * HALF_WIN
STAT_HEADS = 4
LSE_REP = LANES // STAT_HEADS


def _attn_b_kernel(*refs, merged, seq_len):
    it = iter(refs)
    q_ref, kl_ref, km_ref, kr_ref, vl_ref, vm_ref, vr_ref, bb_ref = (next(it) for _ in range(8))
    parts = [(next(it), next(it)) for _ in merged]
    o_ref = next(it)
    l_ref = None if merged else next(it)
    kbuf, vbuf = next(it), next(it)
    flat = [(next(it), next(it)) for _ in merged]

    tq = q_ref.shape[0]
    n_heads = q_ref.shape[1] // HEAD_DIM
    n_sub = tq // SUB
    i = pl.program_id(3)
    kbuf[0:HALF_WIN, :] = kl_ref[...]
    kbuf[HALF_WIN:HALF_WIN + tq, :] = km_ref[...]
    kbuf[HALF_WIN + tq:, :] = kr_ref[...]
    ones = jnp.ones((tq + 2 * HALF_WIN, HEAD_DIM), BF16)
    for h in range(n_heads):
        c0, w0 = h * HEAD_DIM, 2 * h * HEAD_DIM
        vbuf[0:HALF_WIN, w0:w0 + HEAD_DIM] = vl_ref[:, c0:c0 + HEAD_DIM]
        vbuf[HALF_WIN:HALF_WIN + tq, w0:w0 + HEAD_DIM] = vm_ref[:, c0:c0 + HEAD_DIM]
        vbuf[HALF_WIN + tq:, w0:w0 + HEAD_DIM] = vr_ref[:, c0:c0 + HEAD_DIM]
        vbuf[:, w0 + HEAD_DIM:w0 + 2 * HEAD_DIM] = ones

    for d, (po_ref, pl_ref), (fo_ref, fl_ref) in zip(merged, parts, flat):
        for r in range(d):
            fl_ref[pl.ds(r, tq // d, stride=d), :] = pl_ref[r]
            for h in range(n_heads):
                fo_ref[h, pl.ds(r, tq // d, stride=d), :] = (
                    po_ref[r, :, h * HEAD_DIM:(h + 1) * HEAD_DIM].astype(F32))

    col = lax.broadcasted_iota(jnp.int32, (1, WIN), 1)
    lane_head = lax.broadcasted_iota(jnp.int32, (SUB, LANES), 1) // LSE_REP
    for sub in range(n_sub):
        r0 = sub * SUB
        kidx = i * tq + (r0 - HALF_WIN) + col
        colmask = jnp.where((kidx >= 0) & (kidx < seq_len), 0.0, NEG_INF).astype(F32)
        stats = [jnp.zeros((SUB, LANES), F32)] * (n_heads // STAT_HEADS)
        for h in range(n_heads):
            c0 = h * HEAD_DIM
            q = q_ref[r0:r0 + SUB, c0:c0 + HEAD_DIM]
            k = kbuf[r0:r0 + WIN, c0:c0 + HEAD_DIM]
            s = lax.dot_general(q, k, (((1,), (1,)), ((), ())), preferred_element_type=F32)
            s = s + bb_ref[h]
            if sub == 0 or sub == n_sub - 1:
                s = s + colmask
            m = jnp.max(s, axis=-1, keepdims=True)
            p = jnp.exp2(s - m).astype(BF16)
            pv = jnp.dot(p, vbuf[r0:r0 + WIN, 2 * c0:2 * c0 + 2 * HEAD_DIM], preferred_element_type=F32)
            l = pv[:, HEAD_DIM:]
            o = pv[:, 0:HEAD_DIM] / l
            lse = m + jnp.log2(l)
            if merged:
                lses = [lse] + [fl_ref[r0:r0 + SUB, h * LSE_REP:h * LSE_REP + 1] for _, fl_ref in flat]
                outs = [o] + [fo_ref[h, r0:r0 + SUB, :] for fo_ref, _ in flat]
                top = functools.reduce(jnp.maximum, lses)
                ws = [jnp.exp2(x - top) for x in lses]
                o = sum(w * x for w, x in zip(ws, outs)) / sum(ws)
            else:
                st = h // STAT_HEADS
                stats[st] = jnp.where(lane_head == h % STAT_HEADS, lse, stats[st])
            o_ref[r0:r0 + SUB, c0:c0 + HEAD_DIM] = o.astype(o_ref.dtype)
        if not merged:
            for st, tile in enumerate(stats):
                l_ref[r0:r0 + SUB, st * LANES:(st + 1) * LANES] = tile


def _band_bias(d):
    a = np.arange(SUB)[:, None]
    c = np.arange(WIN)[None, :]
    off = c - HALF_WIN - a
    slopes = (2.0 ** -np.arange(1, B_HEADS + 1)).astype(np.float32)
    bias = -slopes[:, None, None] * (np.abs(off) * d).astype(np.float32)[None] * np.float32(LOG2E)
    return np.where((np.abs(off) <= HALF_WIN)[None], bias, np.float32(NEG_INF)).astype(np.float32)


def _attn_b_pattern(src, cols, d, partials, *, hb_heads, tq):
    b, _, seq_len, _ = src.shape
    col_q, col_k, col_v = cols
    tq = min(tq, seq_len)
    hb_w = hb_heads * HEAD_DIM
    stat_w = hb_heads // STAT_HEADS * LANES
    assert all(c % hb_w == 0 for c in cols) and hb_heads % STAT_HEADS == 0
    nblk = seq_len // tq
    halo_per_blk = tq // HALF_WIN
    n_halo = seq_len // HALF_WIN
    n_hb = B_HEADS // hb_heads
    merged = tuple(dd for dd, _, _ in partials)
    assert not merged or hb_heads == STAT_HEADS
    bb = jnp.asarray(_band_bias(d))

    def main(col0, width=hb_w):
        return pl.BlockSpec((None, None, tq, width),
                            lambda bi, r, hb, i: (bi, r, i, col0 // width + hb))

    def left(col0):
        return pl.BlockSpec((None, None, HALF_WIN, hb_w),
                            lambda bi, r, hb, i: (bi, r, jnp.maximum(i * halo_per_blk - 1, 0),
                                                  col0 // hb_w + hb))

    def right(col0):
        return pl.BlockSpec((None, None, HALF_WIN, hb_w),
                            lambda bi, r, hb, i: (bi, r, jnp.minimum((i + 1) * halo_per_blk, n_halo - 1),
                                                  col0 // hb_w + hb))

    def part(dd, width):
        return pl.BlockSpec((None, dd, tq // dd, width), lambda bi, r, hb, i: (bi, 0, i, hb))

    in_specs = [main(col_q), left(col_k), main(col_k), right(col_k),
                left(col_v), main(col_v), right(col_v),
                pl.BlockSpec((hb_heads, SUB, WIN), lambda bi, r, hb, i: (hb, 0, 0))]
    args = [src] * 7 + [bb]
    scratch = [pltpu.VMEM((tq + 2 * HALF_WIN, hb_w), BF16), pltpu.VMEM((tq + 2 * HALF_WIN, 2 * hb_w), BF16)]
    for dd, o_part, l_part in partials:
        in_specs += [part(dd, hb_w), part(dd, LANES)]
        args += [o_part, l_part]
        scratch += [pltpu.VMEM((hb_heads, tq, LANES), F32), pltpu.VMEM((tq, LANES), F32)]
    if merged:
        out_shape = jax.ShapeDtypeStruct((b, d, seq_len, D_B), BF16)
        out_specs = main(0)
    else:
        out_shape = (jax.ShapeDtypeStruct((b, d, seq_len, D_B), BF16),
                     jax.ShapeDtypeStruct((b, d, seq_len, B_HEADS // STAT_HEADS * LANES), F32))
        out_specs = (main(0), main(0, stat_w))
    return pl.pallas_call(
        functools.partial(_attn_b_kernel, merged=merged, seq_len=seq_len),
        out_shape=out_shape,
        grid=(b, d, n_hb, nblk),
        in_specs=in_specs,
        out_specs=out_specs,
        scratch_shapes=scratch,
        compiler_params=_cparams(("parallel", "parallel", "parallel", "arbitrary")),
        name=f"attn_b_d{d}",
    )(*args)


def _attn_b(proj3, strided):
    b, t, _ = proj3.shape
    assert all(w // (2 * d) == HALF_WIN for w, d in DILATED_PATTERNS) and DILATED_PATTERNS[0][1] == 1
    partials = []
    for d, src in zip(STRIDES, strided):
        o_part, l_part = _attn_b_pattern(src, (0, D_B, 2 * D_B), d, [], hb_heads=B_HEADS, tq=1024)
        partials.append((d, o_part, l_part))
    y = _attn_b_pattern(proj3.reshape(b, 1, t, D_IN), (COL_QB, COL_KB, COL_VB), 1, partials,
                        hb_heads=STAT_HEADS, tq=1024)
    return y.reshape(b, t, D_B)


OUT_CHUNK = 512


def _gated(y_ref, g_refs, gain_ref, rows):
    y = y_ref[rows, :].astype(F32)
    ms = jnp.mean(y * y, axis=-1, keepdims=True)
    yn = y * lax.rsqrt(ms + EPS) * gain_ref[...]
    g = jnp.concatenate([r[rows, :] for r in g_refs], axis=-1).astype(F32)
    return (yn * (g * jax.nn.sigmoid(g))).astype(BF16)


def _outproj_kernel(x_ref, ya_ref, yb_ref, ga0_ref, ga1_ref, gb0_ref, gb1_ref, na_ref, nb_ref,
                    w_ref, fn_ref, o_ref, *h_ref):
    for ch in range(x_ref.shape[0] // OUT_CHUNK):
        rows = slice(ch * OUT_CHUNK, (ch + 1) * OUT_CHUNK)
        za = _gated(ya_ref, (ga0_ref, ga1_ref), na_ref, rows)
        zb = _gated(yb_ref, (gb0_ref, gb1_ref), nb_ref, rows)
        out = x_ref[rows, :]
        out = out + jnp.dot(za, w_ref[0:D_A, :], preferred_element_type=F32)
        out = out + jnp.dot(zb, w_ref[D_A:, :], preferred_element_type=F32)
        ms = jnp.mean(out * out, axis=-1, keepdims=True)
        normed = out * lax.rsqrt(ms + EPS) * fn_ref[...]
        if h_ref:
            o_ref[rows, :] = out
            h_ref[0][rows, :] = normed.astype(h_ref[0].dtype)
        else:
            o_ref[rows, :] = normed


def _outproj(x2, ya2, yb2, proj2, na, nb, w, fn, *, layer, final, tm=512):
    m = x2.shape[0]
    half = D_A // 2

    def gate(col0, k):
        return pl.BlockSpec((tm, half), lambda i: (i, col0 // half + k))

    row = lambda width: pl.BlockSpec((tm, width), lambda i: (i, 0))
    const = lambda shape: pl.BlockSpec(shape, lambda i: (0, 0))
    stream = jax.ShapeDtypeStruct((m, D_MODEL), F32)
    return pl.pallas_call(
        _outproj_kernel,
        out_shape=stream if final else (stream, jax.ShapeDtypeStruct((m, D_MODEL), BF16)),
        grid=(m // tm,),
        in_specs=[row(D_MODEL), row(D_A), row(D_B),
                  gate(COL_GA, 0), gate(COL_GA, 1), gate(COL_GB, 0), gate(COL_GB, 1),
                  const((1, D_A)), const((1, D_B)),
                  pl.BlockSpec((None, D_A + D_B, D_MODEL), lambda i: (layer, 0, 0)), const((1, D_MODEL))],
        out_specs=row(D_MODEL) if final else (row(D_MODEL), row(D_MODEL)),
        compiler_params=_cparams(("parallel",)),
        name="outproj",
    )(x2, ya2, yb2, proj2, proj2, proj2, proj2, na, nb, w, fn)


def _rope_tables(t):
    pos = jnp.arange(t, dtype=jnp.int32)
    row = (pos // GRID_W).astype(F32)
    col = (pos % GRID_W).astype(F32)
    inv_freq = ROPE_THETA ** (-jnp.arange(0, AXIS_DIM, 2, dtype=F32) / AXIS_DIM)
    ang_r = row[:, None] * inv_freq[None, :]
    ang_c = col[:, None] * inv_freq[None, :]
    cos = jnp.concatenate([jnp.cos(ang_r)] * 2 + [jnp.cos(ang_c)] * 2, axis=-1)
    sin = jnp.concatenate([-jnp.sin(ang_r), jnp.sin(ang_r), -jnp.sin(ang_c), jnp.sin(ang_c)], axis=-1)
    return cos, sin


def kernel(x, norm_w, w_in, q_norm_a, k_norm_a, out_norm_a, out_norm_b, w_out, final_norm):
    b, t, d_model = x.shape
    depth = w_in.shape[0]
    assert d_model == D_MODEL and w_in.shape[1:] == (D_MODEL, D_IN)
    assert t % (DILATED_PATTERNS[-1][1] * SUB) == 0
    cos, sin = _rope_tables(t)
    col_scale = jnp.ones((D_IN,), F32).at[COL_QB:COL_KB].set(SCALE * LOG2E)
    w_in_bf = (w_in * col_scale).astype(BF16)
    w_out_bf = w_out.astype(BF16)
    x2 = x.reshape(b * t, D_MODEL)
    h2 = _norm(x2, norm_w[0][None])
    for l in range(depth):
        final = l == depth - 1
        proj2, *strided = _inproj(h2, w_in_bf, (q_norm_a[l] * (SCALE * LOG2E))[None],
                                  k_norm_a[l][None], cos, sin, layer=l, seq=t)
        proj3 = proj2.reshape(b, t, D_IN)
        ya = _attn_a(proj3)
        yb = _attn_b(proj3, strided)
        out = _outproj(x2, ya.reshape(b * t, D_A), yb.reshape(b * t, D_B), proj2,
                       out_norm_a[l][None], out_norm_b[l][None], w_out_bf,
                       (final_norm if final else norm_w[l + 1])[None], layer=l, final=final)
        x2, h2 = (out, None) if final else out
    return x2.reshape(b, t, D_MODEL)
```

```python
import functools

import numpy as np
import jax
import jax.numpy as jnp
from jax import lax
from jax.experimental import pallas as pl
from jax.experimental.pallas import tpu as pltpu

D_MODEL = 2048
HEAD_DIM = 128
A_HEADS = 8
A_KV_HEADS = 2
A_GROUP = A_HEADS // A_KV_HEADS
B_HEADS = 8
D_A = A_HEADS * HEAD_DIM
D_B = B_HEADS * HEAD_DIM
KV_DIM = A_KV_HEADS * HEAD_DIM
D_IN = 2 * D_A + 2 * KV_DIM + 4 * D_B
GRID_W = 64
AXIS_DIM = HEAD_DIM // 2
ROPE_THETA = 10000.0
DILATED_PATTERNS = ((128, 1), (512, 4), (2048, 16))
HALF_WIN = 64
SCALE = HEAD_DIM ** -0.5
LOG2E = float(np.log2(np.e))
EPS = 1e-6
NEG_INF = -1e30

COL_QA, COL_KA, COL_VA, COL_GA = 0, D_A, D_A + KV_DIM, D_A + 2 * KV_DIM
COL_QB = COL_GA + D_A
COL_KB, COL_VB, COL_GB = COL_QB + D_B, COL_QB + 2 * D_B, COL_QB + 3 * D_B

LANES = 128
VMEM_LIMIT = 56 * 1024 * 1024

BF16 = jnp.bfloat16
F32 = jnp.float32


def _cparams(sem):
    return pltpu.CompilerParams(dimension_semantics=sem, vmem_limit_bytes=VMEM_LIMIT)


def _rope_group(a, gain, cos, sin):
    ms = jnp.mean(a * a, axis=-1, keepdims=True)
    y = a * lax.rsqrt(ms + EPS) * gain
    lane = lax.broadcasted_iota(jnp.int32, y.shape, 1)
    partner = jnp.where((lane % AXIS_DIM) < AXIS_DIM // 2,
                        pltpu.roll(y, LANES - AXIS_DIM // 2, 1), pltpu.roll(y, AXIS_DIM // 2, 1))
    return y * cos + partner * sin


STRIDES = tuple(d for _, d in DILATED_PATTERNS if d > 1)
QKVB_W = COL_GB - COL_QB
ROW_CHUNK = 256
INPROJ_TN = 512
N_SLABS = 2


def _inproj_kernel(h_ref, w_ref, qg_ref, kg_ref, cos_ref, sin_ref, o_ref, *rest, tn):
    strided_refs, (slab_ref, slab2_ref) = rest[:len(STRIDES)], rest[len(STRIDES):]
    j = pl.program_id(1)
    tm = h_ref.shape[0]
    n_chunks = tm // ROW_CHUNK
    groups_per_tile = tn // HEAD_DIM
    n_q, n_k = D_A // HEAD_DIM, KV_DIM // HEAD_DIM
    n_special = -(-(n_q + n_k) // groups_per_tile)
    first_b, end_b = COL_QB // tn, COL_GB // tn

    def tile_body(kinds, strided):
        for ch in range(n_chunks):
            rows = slice(ch * ROW_CHUNK, (ch + 1) * ROW_CHUNK)
            acc = jnp.dot(h_ref[rows, :], w_ref[...], preferred_element_type=F32)
            for c, kind in enumerate(kinds):
                a = acc[:, c * HEAD_DIM:(c + 1) * HEAD_DIM]
                if kind is not None:
                    gain = qg_ref[...] if kind == 'q' else kg_ref[...]
                    a = _rope_group(a, gain, cos_ref[rows, :], sin_ref[rows, :])
                o_ref[rows, c * HEAD_DIM:(c + 1) * HEAD_DIM] = a.astype(o_ref.dtype)
                if strided:
                    slab_ref[ch % N_SLABS, c] = a
            if strided:
                (d1, ref1), (d2, ref2) = zip(STRIDES, strided_refs)
                n1, n2, step = ROW_CHUNK // d1, ROW_CHUNK // d2, d2 // d1
                for c in range(groups_per_tile):
                    cols = slice(c * LANES, (c + 1) * LANES)
                    for r1 in range(d1):
                        v = slab_ref[ch % N_SLABS, c, pl.ds(r1, n1, stride=d1), :]
                        ref1[r1, ch * n1:(ch + 1) * n1, cols] = v.astype(ref1.dtype)
                        slab2_ref[ch % N_SLABS, c, r1] = v
                    for r1 in range(d1):
                        for q2 in range(step):
                            v = slab2_ref[ch % N_SLABS, c, r1, pl.ds(q2, n2, stride=step), :]
                            ref2[q2 * d1 + r1, ch * n2:(ch + 1) * n2, cols] = v.astype(ref2.dtype)

    def kinds_of(tile):
        gs = [tile * groups_per_tile + c for c in range(groups_per_tile)]
        return tuple('q' if g < n_q else 'k' if g < n_q + n_k else None for g in gs)

    special = {}
    for tile in range(n_special):
        special.setdefault(kinds_of(tile), []).append(tile)
    for kinds, tiles in special.items():
        pl.when((j >= tiles[0]) & (j <= tiles[-1]))(functools.partial(tile_body, kinds, False))
    plain = (None,) * groups_per_tile
    pl.when((j >= first_b) & (j < end_b))(functools.partial(tile_body, plain, True))
    pl.when((j >= n_special) & ((j < first_b) | (j >= end_b)))(functools.partial(tile_body, plain, False))


def _norm_kernel(x_ref, nw_ref, h_ref):
    x = x_ref[...]
    ms = jnp.mean(x * x, axis=-1, keepdims=True)
    h_ref[...] = (x * lax.rsqrt(ms + EPS) * nw_ref[...]).astype(h_ref.dtype)


def _norm(x2, nw, *, tm=1024):
    m = x2.shape[0]
    return pl.pallas_call(
        _norm_kernel,
        out_shape=jax.ShapeDtypeStruct((m, D_MODEL), BF16),
        grid=(m // tm,),
        in_specs=[pl.BlockSpec((tm, D_MODEL), lambda i: (i, 0)), pl.BlockSpec((1, D_MODEL), lambda i: (0, 0))],
        out_specs=pl.BlockSpec((tm, D_MODEL), lambda i: (i, 0)),
        compiler_params=_cparams(("parallel",)),
        name="norm",
    )(x2, nw)


def _inproj(h2, w, qg, kg, cos, sin, *, layer, seq, tm=2048, tn=INPROJ_TN):
    m = h2.shape[0]
    tm = min(tm, seq)
    t_blocks = seq // tm
    batch = m // seq
    assert COL_QB % tn == 0 and COL_GB % tn == 0
    assert len(STRIDES) == 2 and STRIDES[1] % STRIDES[0] == 0 and ROW_CHUNK % STRIDES[1] == 0
    first_b, n_b = COL_QB // tn, QKVB_W // tn

    def strided_spec(d):
        return pl.BlockSpec((None, d, tm // d, tn),
                            lambda i, j: (i // t_blocks, 0, i % t_blocks, jnp.clip(j - first_b, 0, n_b - 1)))

    return pl.pallas_call(
        functools.partial(_inproj_kernel, tn=tn),
        out_shape=(jax.ShapeDtypeStruct((m, D_IN), BF16),)
        + tuple(jax.ShapeDtypeStruct((batch, d, seq // d, QKVB_W), BF16) for d in STRIDES),
        grid=(m // tm, D_IN // tn),
        in_specs=[
            pl.BlockSpec((tm, D_MODEL), lambda i, j: (i, 0)),
            pl.BlockSpec((None, D_MODEL, tn), lambda i, j: (layer, 0, j)),
            pl.BlockSpec((1, HEAD_DIM), lambda i, j: (0, 0)),
            pl.BlockSpec((1, HEAD_DIM), lambda i, j: (0, 0)),
            pl.BlockSpec((tm, HEAD_DIM), lambda i, j: (i % t_blocks, 0)),
            pl.BlockSpec((tm, HEAD_DIM), lambda i, j: (i % t_blocks, 0)),
        ],
        out_specs=(pl.BlockSpec((tm, tn), lambda i, j: (i, j)),) + tuple(strided_spec(d) for d in STRIDES),
        scratch_shapes=[pltpu.VMEM((N_SLABS, tn // LANES, ROW_CHUNK, LANES), F32),
                        pltpu.VMEM((N_SLABS, tn // LANES, STRIDES[0], ROW_CHUNK // STRIDES[0], LANES), F32)],
        compiler_params=_cparams(("arbitrary", "arbitrary")),
        name="inproj",
    )(h2, w, qg, kg, cos, sin)


ONES_ROWS = 16
KV_UNROLL = 4


def _attn_a_kernel(q_ref, k_ref, v_ref, o_ref, vt_ref, s0_ref, s1_ref, acc_ref, qt_ref, *, tk, tq):
    n_q = q_ref.shape[0] // tq
    n_kv = k_ref.shape[0] // tk
    assert n_kv % 2 == 0

    @pl.when(pl.program_id(2) == 0)
    def _():
        for kk in range(n_kv):
            vt_ref[kk, 0:HEAD_DIM, :] = v_ref[kk * tk:(kk + 1) * tk, :].astype(F32).T.astype(BF16)
            vt_ref[kk, HEAD_DIM:, :] = jnp.ones((ONES_ROWS, tk), BF16)

    sbuf = (s0_ref, s1_ref)
    for qi in range(n_q):
        for hh in range(A_GROUP):
            qt_ref[qi, hh] = (q_ref[qi * tq:(qi + 1) * tq, hh * HEAD_DIM:(hh + 1) * HEAD_DIM]
                              .astype(F32).T.astype(BF16))

    def scores(qi, kk, dst):
        start = pl.multiple_of(kk * tk, tk)
        k = k_ref[pl.ds(start, tk), :]
        for hh in range(A_GROUP):
            dst[hh] = jnp.dot(k, qt_ref[qi, hh], preferred_element_type=F32)

    def softmax_pv(kk, src, ml):
        vt = vt_ref[kk]
        new = []
        for hh in range(A_GROUP):
            m, l = ml[hh]
            s = src[hh]
            m_new = jnp.maximum(m, jnp.max(s, axis=0, keepdims=True))
            alpha = jnp.exp2(m - m_new)
            p = jnp.exp2(s - m_new).astype(BF16)
            pv = jnp.dot(vt, p, preferred_element_type=F32)
            acc_ref[hh] = alpha * acc_ref[hh] + pv[0:HEAD_DIM]
            new.append((m_new, alpha * l + pv[HEAD_DIM:HEAD_DIM + 1]))
        return tuple(new)

    unroll = min(KV_UNROLL, n_kv)

    def trip(qi, kk0, ml, last):
        for u in range(unroll):
            if last and u == unroll - 1:
                scores(jnp.minimum(qi + 1, n_q - 1), 0, sbuf[(u + 1) % 2])
            else:
                scores(qi, kk0 + u + 1, sbuf[(u + 1) % 2])
            ml = softmax_pv(kk0 + u, sbuf[u % 2], ml)
        return ml

    def query_tile(qi, carry):
        acc_ref[...] = jnp.zeros_like(acc_ref)
        ml = tuple((jnp.full((1, tq), -jnp.inf, F32), jnp.zeros((1, tq), F32)) for _ in range(A_GROUP))
        ml = lax.fori_loop(0, n_kv // unroll - 1, lambda jj, ml: trip(qi, jj * unroll, ml, False), ml)
        ml = trip(qi, n_kv - unroll, ml, True)
        rows = pl.ds(pl.multiple_of(qi * tq, tq), tq)
        for hh in range(A_GROUP):
            o_ref[rows, hh * HEAD_DIM:(hh + 1) * HEAD_DIM] = (acc_ref[hh] / ml[hh][1]).T.astype(o_ref.dtype)
        return carry

    scores(0, 0, sbuf[0])
    lax.fori_loop(0, n_q, query_tile, 0)


def _attn_a(proj3, *, tq=256, n_q=8, tk=512):
    b, t, _ = proj3.shape
    gw = A_GROUP * HEAD_DIM
    rows = n_q * tq
    return pl.pallas_call(
        functools.partial(_attn_a_kernel, tk=tk, tq=tq),
        out_shape=jax.ShapeDtypeStruct((b, t, D_A), BF16),
        grid=(b, A_KV_HEADS, t // rows),
        in_specs=[
            pl.BlockSpec((None, rows, gw), lambda bi, g, qi: (bi, qi, COL_QA // gw + g)),
            pl.BlockSpec((None, t, HEAD_DIM), lambda bi, g, qi: (bi, 0, COL_KA // HEAD_DIM + g)),
            pl.BlockSpec((None, t, HEAD_DIM), lambda bi, g, qi: (bi, 0, COL_VA // HEAD_DIM + g)),
        ],
        out_specs=pl.BlockSpec((None, rows, gw), lambda bi, g, qi: (bi, qi, g)),
        scratch_shapes=[pltpu.VMEM((t // tk, HEAD_DIM + ONES_ROWS, tk), BF16),
                        pltpu.VMEM((A_GROUP, tk, tq), F32), pltpu.VMEM((A_GROUP, tk, tq), F32),
                        pltpu.VMEM((A_GROUP, HEAD_DIM, tq), F32),
                        pltpu.VMEM((n_q, A_GROUP, HEAD_DIM, tq), BF16)],
        compiler_params=_cparams(("parallel", "parallel", "arbitrary")),
        name="attn_a",
    )(proj3, proj3, proj3)


SUB = 128
WIN = SUB + 2 * HALF_WIN
STAT_HEADS = 4
LSE_REP = LANES // STAT_HEADS


def _attn_b_kernel(*refs, merged, seq_len):
    it = iter(refs)
    q_ref, kl_ref, km_ref, kr_ref, vl_ref, vm_ref, vr_ref, bb_ref = (next(it) for _ in range(8))
    parts = [(next(it), next(it)) for _ in merged]
    o_ref = next(it)
    l_ref = None if merged else next(it)
    kbuf, vbuf = next(it), next(it)
    flat = [(next(it), next(it)) for _ in merged]

    tq = q_ref.shape[0]
    n_heads = q_ref.shape[1] // HEAD_DIM
    n_sub = tq // SUB
    i = pl.program_id(3)
    kbuf[0:HALF_WIN, :] = kl_ref[...]
    kbuf[HALF_WIN:HALF_WIN + tq, :] = km_ref[...]
    kbuf[HALF_WIN + tq:, :] = kr_ref[...]
    ones = jnp.ones((tq + 2 * HALF_WIN, HEAD_DIM), BF16)
    for h in range(n_heads):
        c0, w0 = h * HEAD_DIM, 2 * h * HEAD_DIM
        vbuf[0:HALF_WIN, w0:w0 + HEAD_DIM] = vl_ref[:, c0:c0 + HEAD_DIM]
        vbuf[HALF_WIN:HALF_WIN + tq, w0:w0 + HEAD_DIM] = vm_ref[:, c0:c0 + HEAD_DIM]
        vbuf[HALF_WIN + tq:, w0:w0 + HEAD_DIM] = vr_ref[:, c0:c0 + HEAD_DIM]
        vbuf[:, w0 + HEAD_DIM:w0 + 2 * HEAD_DIM] = ones

    for d, (po_ref, pl_ref), (fo_ref, fl_ref) in zip(merged, parts, flat):
        for r in range(d):
            fl_ref[pl.ds(r, tq // d, stride=d), :] = pl_ref[r]
            for h in range(n_heads):
                fo_ref[h, pl.ds(r, tq // d, stride=d), :] = (
                    po_ref[r, :, h * HEAD_DIM:(h + 1) * HEAD_DIM].astype(F32))

    col = lax.broadcasted_iota(jnp.int32, (1, WIN), 1)
    lane_head = lax.broadcasted_iota(jnp.int32, (SUB, LANES), 1) // LSE_REP
    for sub in range(n_sub):
        r0 = sub * SUB
        kidx = i * tq + (r0 - HALF_WIN) + col
        colmask = jnp.where((kidx >= 0) & (kidx < seq_len), 0.0, NEG_INF).astype(F32)
        stats = [jnp.zeros((SUB, LANES), F32)] * (n_heads // STAT_HEADS)
        for h in range(n_heads):
            c0 = h * HEAD_DIM
            q = q_ref[r0:r0 + SUB, c0:c0 + HEAD_DIM]
            k = kbuf[r0:r0 + WIN, c0:c0 + HEAD_DIM]
            s = lax.dot_general(q, k, (((1,), (1,)), ((), ())), preferred_element_type=F32)
            s = s + bb_ref[h]
            if sub == 0 or sub == n_sub - 1:
                s = s + colmask
            m = jnp.max(s, axis=-1, keepdims=True)
            p = jnp.exp2(s - m).astype(BF16)
            pv = jnp.dot(p, vbuf[r0:r0 + WIN, 2 * c0:2 * c0 + 2 * HEAD_DIM], preferred_element_type=F32)
            l = pv[:, HEAD_DIM:]
            o = pv[:, 0:HEAD_DIM] / l
            lse = m + jnp.log2(l)
            if merged:
                lses = [lse] + [fl_ref[r0:r0 + SUB, h * LSE_REP:h * LSE_REP + 1] for _, fl_ref in flat]
                outs = [o] + [fo_ref[h, r0:r0 + SUB, :] for fo_ref, _ in flat]
                top = functools.reduce(jnp.maximum, lses)
                ws = [jnp.exp2(x - top) for x in lses]
                o = sum(w * x for w, x in zip(ws, outs)) / sum(ws)
            else:
                st = h // STAT_HEADS
                stats[st] = jnp.where(lane_head == h % STAT_HEADS, lse, stats[st])
            o_ref[r0:r0 + SUB, c0:c0 + HEAD_DIM] = o.astype(o_ref.dtype)
        if not merged:
            for st, tile in enumerate(stats):
                l_ref[r0:r0 + SUB, st * LANES:(st + 1) * LANES] = tile


def _band_bias(d):
    a = np.arange(SUB)[:, None]
    c = np.arange(WIN)[None, :]
    off = c - HALF_WIN - a
    slopes = (2.0 ** -np.arange(1, B_HEADS + 1)).astype(np.float32)
    bias = -slopes[:, None, None] * (np.abs(off) * d).astype(np.float32)[None] * np.float32(LOG2E)
    return np.where((np.abs(off) <= HALF_WIN)[None], bias, np.float32(NEG_INF)).astype(np.float32)


def _attn_b_pattern(src, cols, d, partials, *, hb_heads, tq):
    b, _, seq_len, _ = src.shape
    col_q, col_k, col_v = cols
    tq = min(tq, seq_len)
    hb_w = hb_heads * HEAD_DIM
    stat_w = hb_heads // STAT_HEADS * LANES
    assert all(c % hb_w == 0 for c in cols) and hb_heads % STAT_HEADS == 0
    nblk = seq_len // tq
    halo_per_blk = tq // HALF_WIN
    n_halo = seq_len // HALF_WIN
    n_hb = B_HEADS // hb_heads
    merged = tuple(dd for dd, _, _ in partials)
    assert not merged or hb_heads == STAT_HEADS
    bb = jnp.asarray(_band_bias(d))

    def main(col0, width=hb_w):
        return pl.BlockSpec((None, None, tq, width),
                            lambda bi, r, hb, i: (bi, r, i, col0 // width + hb))

    def left(col0):
        return pl.BlockSpec((None, None, HALF_WIN, hb_w),
                            lambda bi, r, hb, i: (bi, r, jnp.maximum(i * halo_per_blk - 1, 0),
                                                  col0 // hb_w + hb))

    def right(col0):
        return pl.BlockSpec((None, None, HALF_WIN, hb_w),
                            lambda bi, r, hb, i: (bi, r, jnp.minimum((i + 1) * halo_per_blk, n_halo - 1),
                                                  col0 // hb_w + hb))

    def part(dd, width):
        return pl.BlockSpec((None, dd, tq // dd, width), lambda bi, r, hb, i: (bi, 0, i, hb))

    in_specs = [main(col_q), left(col_k), main(col_k), right(col_k),
                left(col_v), main(col_v), right(col_v),
                pl.BlockSpec((hb_heads, SUB, WIN), lambda bi, r, hb, i: (hb, 0, 0))]
    args = [src] * 7 + [bb]
    scratch = [pltpu.VMEM((tq + 2 * HALF_WIN, hb_w), BF16), pltpu.VMEM((tq + 2 * HALF_WIN, 2 * hb_w), BF16)]
    for dd, o_part, l_part in partials:
        in_specs += [part(dd, hb_w), part(dd, LANES)]
        args += [o_part, l_part]
        scratch += [pltpu.VMEM((hb_heads, tq, LANES), F32), pltpu.VMEM((tq, LANES), F32)]
    if merged:
        out_shape = jax.ShapeDtypeStruct((b, d, seq_len, D_B), BF16)
        out_specs = main(0)
    else:
        out_shape = (jax.ShapeDtypeStruct((b, d, seq_len, D_B), BF16),
                     jax.ShapeDtypeStruct((b, d, seq_len, B_HEADS // STAT_HEADS * LANES), F32))
        out_specs = (main(0), main(0, stat_w))
    return pl.pallas_call(
        functools.partial(_attn_b_kernel, merged=merged, seq_len=seq_len),
        out_shape=out_shape,
        grid=(b, d, n_hb, nblk),
        in_specs=in_specs,
        out_specs=out_specs,
        scratch_shapes=scratch,
        compiler_params=_cparams(("parallel", "parallel", "parallel", "arbitrary")),
        name=f"attn_b_d{d}",
    )(*args)


def _attn_b(proj3, strided):
    b, t, _ = proj3.shape
    assert all(w // (2 * d) == HALF_WIN for w, d in DILATED_PATTERNS) and DILATED_PATTERNS[0][1] == 1
    partials = []
    for d, src in zip(STRIDES, strided):
        o_part, l_part = _attn_b_pattern(src, (0, D_B, 2 * D_B), d, [], hb_heads=B_HEADS, tq=1024)
        partials.append((d, o_part, l_part))
    y = _attn_b_pattern(proj3.reshape(b, 1, t, D_IN), (COL_QB, COL_KB, COL_VB), 1, partials,
                        hb_heads=STAT_HEADS, tq=1024)
    return y.reshape(b, t, D_B)


OUT_CHUNK = 512


def _gated(y_ref, g_refs, gain_ref, rows):
    y = y_ref[rows, :].astype(F32)
    ms = jnp.mean(y * y, axis=-1, keepdims=True)
    yn = y * lax.rsqrt(ms + EPS) * gain_ref[...]
    g = jnp.concatenate([r[rows, :] for r in g_refs], axis=-1).astype(F32)
    return (yn * (g * jax.nn.sigmoid(g))).astype(BF16)


def _outproj_kernel(x_ref, ya_ref, yb_ref, ga0_ref, ga1_ref, gb0_ref, gb1_ref, na_ref, nb_ref,
                    w_ref, fn_ref, o_ref, *h_ref):
    for ch in range(x_ref.shape[0] // OUT_CHUNK):
        rows = slice(ch * OUT_CHUNK, (ch + 1) * OUT_CHUNK)
        za = _gated(ya_ref, (ga0_ref, ga1_ref), na_ref, rows)
        zb = _gated(yb_ref, (gb0_ref, gb1_ref), nb_ref, rows)
        out = x_ref[rows, :]
        out = out + jnp.dot(za, w_ref[0:D_A, :], preferred_element_type=F32)
        out = out + jnp.dot(zb, w_ref[D_A:, :], preferred_element_type=F32)
        ms = jnp.mean(out * out, axis=-1, keepdims=True)
        normed = out * lax.rsqrt(ms + EPS) * fn_ref[...]
        if h_ref:
            o_ref[rows, :] = out
            h_ref[0][rows, :] = normed.astype(h_ref[0].dtype)
        else:
            o_ref[rows, :] = normed


def _outproj(x2, ya2, yb2, proj2, na, nb, w, fn, *, layer, final, tm=512):
    m = x2.shape[0]
    half = D_A // 2

    def gate(col0, k):
        return pl.BlockSpec((tm, half), lambda i: (i, col0 // half + k))

    row = lambda width: pl.BlockSpec((tm, width), lambda i: (i, 0))
    const = lambda shape: pl.BlockSpec(shape, lambda i: (0, 0))
    stream = jax.ShapeDtypeStruct((m, D_MODEL), F32)
    return pl.pallas_call(
        _outproj_kernel,
        out_shape=stream if final else (stream, jax.ShapeDtypeStruct((m, D_MODEL), BF16)),
        grid=(m // tm,),
        in_specs=[row(D_MODEL), row(D_A), row(D_B),
                  gate(COL_GA, 0), gate(COL_GA, 1), gate(COL_GB, 0), gate(COL_GB, 1),
                  const((1, D_A)), const((1, D_B)),
                  pl.BlockSpec((None, D_A + D_B, D_MODEL), lambda i: (layer, 0, 0)), const((1, D_MODEL))],
        out_specs=row(D_MODEL) if final else (row(D_MODEL), row(D_MODEL)),
        compiler_params=_cparams(("parallel",)),
        name="outproj",
    )(x2, ya2, yb2, proj2, proj2, proj2, proj2, na, nb, w, fn)


def _rope_tables(t):
    pos = jnp.arange(t, dtype=jnp.int32)
    row = (pos // GRID_W).astype(F32)
    col = (pos % GRID_W).astype(F32)
    inv_freq = ROPE_THETA ** (-jnp.arange(0, AXIS_DIM, 2, dtype=F32) / AXIS_DIM)
    ang_r = row[:, None] * inv_freq[None, :]
    ang_c = col[:, None] * inv_freq[None, :]
    cos = jnp.concatenate([jnp.cos(ang_r)] * 2 + [jnp.cos(ang_c)] * 2, axis=-1)
    sin = jnp.concatenate([-jnp.sin(ang_r), jnp.sin(ang_r), -jnp.sin(ang_c), jnp.sin(ang_c)], axis=-1)
    return cos, sin


def kernel(x, norm_w, w_in, q_norm_a, k_norm_a, out_norm_a, out_norm_b, w_out, final_norm):
    b, t, d_model = x.shape
    depth = w_in.shape[0]
    assert d_model == D_MODEL and w_in.shape[1:] == (D_MODEL, D_IN)
    assert t % (DILATED_PATTERNS[-1][1] * SUB) == 0
    cos, sin = _rope_tables(t)
    col_scale = jnp.ones((D_IN,), F32).at[COL_QB:COL_KB].set(SCALE * LOG2E)
    w_in_bf = (w_in * col_scale).astype(BF16)
    w_out_bf = w_out.astype(BF16)
    x2 = x.reshape(b * t, D_MODEL)
    h2 = _norm(x2, norm_w[0][None])
    for l in range(depth):
        final = l == depth - 1
        proj2, *strided = _inproj(h2, w_in_bf, (q_norm_a[l] * (SCALE * LOG2E))[None],
                                  k_norm_a[l][None], cos, sin, layer=l, seq=t)
        proj3 = proj2.reshape(b, t, D_IN)
        ya = _attn_a(proj3)
        yb = _attn_b(proj3, strided)
        out = _outproj(x2, ya.reshape(b * t, D_A), yb.reshape(b * t, D_B), proj2,
                       out_norm_a[l][None], out_norm_b[l][None], w_out_bf,
                       (final_norm if final else norm_w[l + 1])[None], layer=l, final=final)
        x2, h2 = (out, None) if final else out
    return x2.reshape(b, t, D_MODEL)
```

```python
import functools

import numpy as np
import jax
import jax.numpy as jnp
from jax import lax
from jax.experimental import pallas as pl
from jax.experimental.pallas import tpu as pltpu

D_MODEL = 2048
HEAD_DIM = 128
A_HEADS = 8
A_KV_HEADS = 2
A_GROUP = A_HEADS // A_KV_HEADS
B_HEADS = 8
D_A = A_HEADS * HEAD_DIM
D_B = B_HEADS * HEAD_DIM
KV_DIM = A_KV_HEADS * HEAD_DIM
D_IN = 2 * D_A + 2 * KV_DIM + 4 * D_B
GRID_W = 64
AXIS_DIM = HEAD_DIM // 2
ROPE_THETA = 10000.0
DILATED_PATTERNS = ((128, 1), (512, 4), (2048, 16))
HALF_WIN = 64
SCALE = HEAD_DIM ** -0.5
LOG2E = float(np.log2(np.e))
EPS = 1e-6
NEG_INF = -1e30

COL_QA, COL_KA, COL_VA, COL_GA = 0, D_A, D_A + KV_DIM, D_A + 2 * KV_DIM
COL_QB = COL_GA + D_A
COL_KB, COL_VB, COL_GB = COL_QB + D_B, COL_QB + 2 * D_B, COL_QB + 3 * D_B

LANES = 128
VMEM_LIMIT = 56 * 1024 * 1024

BF16 = jnp.bfloat16
F32 = jnp.float32


def _cparams(sem):
    return pltpu.CompilerParams(dimension_semantics=sem, vmem_limit_bytes=VMEM_LIMIT)


def _rope_group(a, gain, cos, sin):
    ms = jnp.mean(a * a, axis=-1, keepdims=True)
    y = a * lax.rsqrt(ms + EPS) * gain
    lane = lax.broadcasted_iota(jnp.int32, y.shape, 1)
    partner = jnp.where((lane % AXIS_DIM) < AXIS_DIM // 2,
                        pltpu.roll(y, LANES - AXIS_DIM // 2, 1), pltpu.roll(y, AXIS_DIM // 2, 1))
    return y * cos + partner * sin


STRIDES = tuple(d for _, d in DILATED_PATTERNS if d > 1)
QKVB_W = COL_GB - COL_QB
ROW_CHUNK = 256
INPROJ_TN = 512
N_SLABS = 2


def _inproj_kernel(h_ref, w_ref, qg_ref, kg_ref, cos_ref, sin_ref, o_ref, *rest, tn):
    strided_refs, (slab_ref, slab2_ref) = rest[:len(STRIDES)], rest[len(STRIDES):]
    j = pl.program_id(1)
    tm = h_ref.shape[0]
    n_chunks = tm // ROW_CHUNK
    groups_per_tile = tn // HEAD_DIM
    first_b, end_b = COL_QB // tn, COL_GB // tn

    def tile_body(kinds, strided):
        for ch in range(n_chunks):
            rows = slice(ch * ROW_CHUNK, (ch + 1) * ROW_CHUNK)
            acc = jnp.dot(h_ref[rows, :], w_ref[...], preferred_element_type=F32)
            for c, kind in enumerate(kinds):
                a = acc[:, c * HEAD_DIM:(c + 1) * HEAD_DIM]
                if kind == 'g':
                    a = a * jax.nn.sigmoid(a)
                elif kind is not None:
                    gain = qg_ref[...] if kind == 'q' else kg_ref[...]
                    a = _rope_group(a, gain, cos_ref[rows, :], sin_ref[rows, :])
                o_ref[rows, c * HEAD_DIM:(c + 1) * HEAD_DIM] = a.astype(o_ref.dtype)
                if strided:
                    slab_ref[ch % N_SLABS, c] = a
            if strided:
                (d1, ref1), (d2, ref2) = zip(STRIDES, strided_refs)
                n1, n2, step = ROW_CHUNK // d1, ROW_CHUNK // d2, d2 // d1
                for c in range(groups_per_tile):
                    cols = slice(c * LANES, (c + 1) * LANES)
                    for r1 in range(d1):
                        v = slab_ref[ch % N_SLABS, c, pl.ds(r1, n1, stride=d1), :]
                        ref1[r1, ch * n1:(ch + 1) * n1, cols] = v.astype(ref1.dtype)
                        slab2_ref[ch % N_SLABS, c, r1] = v
                    for r1 in range(d1):
                        for q2 in range(step):
                            v = slab2_ref[ch % N_SLABS, c, r1, pl.ds(q2, n2, stride=step), :]
                            ref2[q2 * d1 + r1, ch * n2:(ch + 1) * n2, cols] = v.astype(ref2.dtype)

    def kind_of(col):
        if col < COL_KA:
            return 'q'
        if col < COL_VA:
            return 'k'
        if COL_GA <= col < COL_QB or col >= COL_GB:
            return 'g'
        return None

    variants = {}
    for tile in range(D_IN // tn):
        kinds = tuple(kind_of(tile * tn + c * HEAD_DIM) for c in range(groups_per_tile))
        variants.setdefault((kinds, first_b <= tile < end_b), []).append(tile)
    for (kinds, strided), tiles in variants.items():
        cond = functools.reduce(jnp.logical_or, [j == t for t in tiles])
        pl.when(cond)(functools.partial(tile_body, kinds, strided))


def _norm_kernel(x_ref, nw_ref, h_ref):
    x = x_ref[...]
    ms = jnp.mean(x * x, axis=-1, keepdims=True)
    h_ref[...] = (x * lax.rsqrt(ms + EPS) * nw_ref[...]).astype(h_ref.dtype)


def _norm(x2, nw, *, tm=1024):
    m = x2.shape[0]
    return pl.pallas_call(
        _norm_kernel,
        out_shape=jax.ShapeDtypeStruct((m, D_MODEL), BF16),
        grid=(m // tm,),
        in_specs=[pl.BlockSpec((tm, D_MODEL), lambda i: (i, 0)), pl.BlockSpec((1, D_MODEL), lambda i: (0, 0))],
        out_specs=pl.BlockSpec((tm, D_MODEL), lambda i: (i, 0)),
        compiler_params=_cparams(("parallel",)),
        name="norm",
    )(x2, nw)


def _inproj(h2, w, qg, kg, cos, sin, *, layer, seq, tm=2048, tn=INPROJ_TN):
    m = h2.shape[0]
    tm = min(tm, seq)
    t_blocks = seq // tm
    batch = m // seq
    assert COL_QB % tn == 0 and COL_GB % tn == 0
    assert len(STRIDES) == 2 and STRIDES[1] % STRIDES[0] == 0 and ROW_CHUNK % STRIDES[1] == 0
    first_b, n_b = COL_QB // tn, QKVB_W // tn

    def strided_spec(d):
        return pl.BlockSpec((None, d, tm // d, tn),
                            lambda i, j: (i // t_blocks, 0, i % t_blocks, jnp.clip(j - first_b, 0, n_b - 1)))

    return pl.pallas_call(
        functools.partial(_inproj_kernel, tn=tn),
        out_shape=(jax.ShapeDtypeStruct((m, D_IN), BF16),)
        + tuple(jax.ShapeDtypeStruct((batch, d, seq // d, QKVB_W), BF16) for d in STRIDES),
        grid=(m // tm, D_IN // tn),
        in_specs=[
            pl.BlockSpec((tm, D_MODEL), lambda i, j: (i, 0)),
            pl.BlockSpec((None, D_MODEL, tn), lambda i, j: (layer, 0, j)),
            pl.BlockSpec((1, HEAD_DIM), lambda i, j: (0, 0)),
            pl.BlockSpec((1, HEAD_DIM), lambda i, j: (0, 0)),
            pl.BlockSpec((tm, HEAD_DIM), lambda i, j: (i % t_blocks, 0)),
            pl.BlockSpec((tm, HEAD_DIM), lambda i, j: (i % t_blocks, 0)),
        ],
        out_specs=(pl.BlockSpec((tm, tn), lambda i, j: (i, j)),) + tuple(strided_spec(d) for d in STRIDES),
        scratch_shapes=[pltpu.VMEM((N_SLABS, tn // LANES, ROW_CHUNK, LANES), F32),
                        pltpu.VMEM((N_SLABS, tn // LANES, STRIDES[0], ROW_CHUNK // STRIDES[0], LANES), F32)],
        compiler_params=_cparams(("arbitrary", "arbitrary")),
        name="inproj",
    )(h2, w, qg, kg, cos, sin)


ONES_ROWS = 16
KV_UNROLL = 4


def _attn_a_kernel(q_ref, k_ref, v_ref, o_ref, vt_ref, s0_ref, s1_ref, acc_ref, qt_ref, *, tk, tq):
    n_q = q_ref.shape[0] // tq
    n_kv = k_ref.shape[0] // tk
    assert n_kv % 2 == 0

    @pl.when(pl.program_id(2) == 0)
    def _():
        for kk in range(n_kv):
            vt_ref[kk, 0:HEAD_DIM, :] = v_ref[kk * tk:(kk + 1) * tk, :].astype(F32).T.astype(BF16)
            vt_ref[kk, HEAD_DIM:, :] = jnp.ones((ONES_ROWS, tk), BF16)

    sbuf = (s0_ref, s1_ref)
    for qi in range(n_q):
        for hh in range(A_GROUP):
            qt_ref[qi, hh] = (q_ref[qi * tq:(qi + 1) * tq, hh * HEAD_DIM:(hh + 1) * HEAD_DIM]
                              .astype(F32).T.astype(BF16))

    def scores(qi, kk, dst):
        start = pl.multiple_of(kk * tk, tk)
        k = k_ref[pl.ds(start, tk), :]
        for hh in range(A_GROUP):
            dst[hh] = jnp.dot(k, qt_ref[qi, hh], preferred_element_type=F32)

    def softmax_pv(kk, src, ml):
        vt = vt_ref[kk]
        new = []
        for hh in range(A_GROUP):
            m, l = ml[hh]
            s = src[hh]
            m_new = jnp.maximum(m, jnp.max(s, axis=0, keepdims=True))
            alpha = jnp.exp2(m - m_new)
            p = jnp.exp2(s - m_new).astype(BF16)
            pv = jnp.dot(vt, p, preferred_element_type=F32)
            acc_ref[hh] = alpha * acc_ref[hh] + pv[0:HEAD_DIM]
            new.append((m_new, alpha * l + pv[HEAD_DIM:HEAD_DIM + 1]))
        return tuple(new)

    unroll = min(KV_UNROLL, n_kv)

    def trip(qi, kk0, ml, last):
        for u in range(unroll):
            if last and u == unroll - 1:
                scores(jnp.minimum(qi + 1, n_q - 1), 0, sbuf[(u + 1) % 2])
            else:
                scores(qi, kk0 + u + 1, sbuf[(u + 1) % 2])
            ml = softmax_pv(kk0 + u, sbuf[u % 2], ml)
        return ml

    def query_tile(qi, carry):
        acc_ref[...] = jnp.zeros_like(acc_ref)
        ml = tuple((jnp.full((1, tq), -jnp.inf, F32), jnp.zeros((1, tq), F32)) for _ in range(A_GROUP))
        ml = lax.fori_loop(0, n_kv // unroll - 1, lambda jj, ml: trip(qi, jj * unroll, ml, False), ml)
        ml = trip(qi, n_kv - unroll, ml, True)
        rows = pl.ds(pl.multiple_of(qi * tq, tq), tq)
        for hh in range(A_GROUP):
            o_ref[rows, hh * HEAD_DIM:(hh + 1) * HEAD_DIM] = (acc_ref[hh] / ml[hh][1]).T.astype(o_ref.dtype)
        return carry

    scores(0, 0, sbuf[0])
    lax.fori_loop(0, n_q, query_tile, 0)


def _attn_a(proj3, *, tq=256, n_q=4, tk=512):
    b, t, _ = proj3.shape
    gw = A_GROUP * HEAD_DIM
    rows = n_q * tq
    return pl.pallas_call(
        functools.partial(_attn_a_kernel, tk=tk, tq=tq),
        out_shape=jax.ShapeDtypeStruct((b, t, D_A), BF16),
        grid=(b, A_KV_HEADS, t // rows),
        in_specs=[
            pl.BlockSpec((None, rows, gw), lambda bi, g, qi: (bi, qi, COL_QA // gw + g)),
            pl.BlockSpec((None, t, HEAD_DIM), lambda bi, g, qi: (bi, 0, COL_KA // HEAD_DIM + g)),
            pl.BlockSpec((None, t, HEAD_DIM), lambda bi, g, qi: (bi, 0, COL_VA // HEAD_DIM + g)),
        ],
        out_specs=pl.BlockSpec((None, rows, gw), lambda bi, g, qi: (bi, qi, g)),
        scratch_shapes=[pltpu.VMEM((t // tk, HEAD_DIM + ONES_ROWS, tk), BF16),
                        pltpu.VMEM((A_GROUP, tk, tq), F32), pltpu.VMEM((A_GROUP, tk, tq), F32),
                        pltpu.VMEM((A_GROUP, HEAD_DIM, tq), F32),
                        pltpu.VMEM((n_q, A_GROUP, HEAD_DIM, tq), BF16)],
        compiler_params=_cparams(("parallel", "parallel", "arbitrary")),
        name="attn_a",
    )(proj3, proj3, proj3)


SUB = 128
WIN = SUB + 2 * HALF_WIN
STAT_HEADS = 4
LSE_REP = LANES // STAT_HEADS


def _attn_b_kernel(*refs, merged, seq_len):
    it = iter(refs)
    q_ref, kl_ref, km_ref, kr_ref, vl_ref, vm_ref, vr_ref, bb_ref = (next(it) for _ in range(8))
    parts = [(next(it), next(it)) for _ in merged]
    o_ref = next(it)
    l_ref = None if merged else next(it)
    kbuf, vbuf = next(it), next(it)
    flat = [(next(it), next(it)) for _ in merged]

    tq = q_ref.shape[0]
    n_heads = q_ref.shape[1] // HEAD_DIM
    n_sub = tq // SUB
    i = pl.program_id(3)
    kbuf[0:HALF_WIN, :] = kl_ref[...]
    kbuf[HALF_WIN:HALF_WIN + tq, :] = km_ref[...]
    kbuf[HALF_WIN + tq:, :] = kr_ref[...]
    ones = jnp.ones((tq + 2 * HALF_WIN, HEAD_DIM), BF16)
    for h in range(n_heads):
        c0, w0 = h * HEAD_DIM, 2 * h * HEAD_DIM
        vbuf[0:HALF_WIN, w0:w0 + HEAD_DIM] = vl_ref[:, c0:c0 + HEAD_DIM]
        vbuf[HALF_WIN:HALF_WIN + tq, w0:w0 + HEAD_DIM] = vm_ref[:, c0:c0 + HEAD_DIM]
        vbuf[HALF_WIN + tq:, w0:w0 + HEAD_DIM] = vr_ref[:, c0:c0 + HEAD_DIM]
        vbuf[:, w0 + HEAD_DIM:w0 + 2 * HEAD_DIM] = ones

    for d, (po_ref, pl_ref), (fo_ref, fl_ref) in zip(merged, parts, flat):
        for r in range(d):
            fl_ref[pl.ds(r, tq // d, stride=d), :] = pl_ref[r]
            for h in range(n_heads):
                fo_ref[h, pl.ds(r, tq // d, stride=d), :] = (
                    po_ref[r, :, h * HEAD_DIM:(h + 1) * HEAD_DIM].astype(F32))

    col = lax.broadcasted_iota(jnp.int32, (1, WIN), 1)
    lane_head = lax.broadcasted_iota(jnp.int32, (SUB, LANES), 1) // LSE_REP
    for sub in range(n_sub):
        r0 = sub * SUB
        kidx = i * tq + (r0 - HALF_WIN) + col
        colmask = jnp.where((kidx >= 0) & (kidx < seq_len), 0.0, NEG_INF).astype(F32)
        stats = [jnp.zeros((SUB, LANES), F32)] * (n_heads // STAT_HEADS)
        for h in range(n_heads):
            c0 = h * HEAD_DIM
            q = q_ref[r0:r0 + SUB, c0:c0 + HEAD_DIM]
            k = kbuf[r0:r0 + WIN, c0:c0 + HEAD_DIM]
            s = lax.dot_general(q, k, (((1,), (1,)), ((), ())), preferred_element_type=F32)
            s = s + bb_ref[h]
            if sub == 0 or sub == n_sub - 1:
                s = s + colmask
            m = jnp.max(s, axis=-1, keepdims=True)
            p = jnp.exp2(s - m).astype(BF16)
            pv = jnp.dot(p, vbuf[r0:r0 + WIN, 2 * c0:2 * c0 + 2 * HEAD_DIM], preferred_element_type=F32)
            l = pv[:, HEAD_DIM:]
            o = pv[:, 0:HEAD_DIM] / l
            lse = m + jnp.log2(l)
            if merged:
                lses = [lse] + [fl_ref[r0:r0 + SUB, h * LSE_REP:h * LSE_REP + 1] for _, fl_ref in flat]
                outs = [o] + [fo_ref[h, r0:r0 + SUB, :] for fo_ref, _ in flat]
                top = functools.reduce(jnp.maximum, lses)
                ws = [jnp.exp2(x - top) for x in lses]
                o = sum(w * x for w, x in zip(ws, outs)) / sum(ws)
            else:
                st = h // STAT_HEADS
                stats[st] = jnp.where(lane_head == h % STAT_HEADS, lse, stats[st])
            o_ref[r0:r0 + SUB, c0:c0 + HEAD_DIM] = o.astype(o_ref.dtype)
        if not merged:
            for st, tile in enumerate(stats):
                l_ref[r0:r0 + SUB, st * LANES:(st + 1) * LANES] = tile


def _band_bias(d):
    a = np.arange(SUB)[:, None]
    c = np.arange(WIN)[None, :]
    off = c - HALF_WIN - a
    slopes = (2.0 ** -np.arange(1, B_HEADS + 1)).astype(np.float32)
    bias = -slopes[:, None, None] * (np.abs(off) * d).astype(np.float32)[None] * np.float32(LOG2E)
    return np.where((np.abs(off) <= HALF_WIN)[None], bias, np.float32(NEG_INF)).astype(np.float32)


def _attn_b_pattern(src, cols, d, partials, *, hb_heads, tq):
    b, _, seq_len, _ = src.shape
    col_q, col_k, col_v = cols
    tq = min(tq, seq_len)
    hb_w = hb_heads * HEAD_DIM
    stat_w = hb_heads // STAT_HEADS * LANES
    assert all(c % hb_w == 0 for c in cols) and hb_heads % STAT_HEADS == 0
    nblk = seq_len // tq
    halo_per_blk = tq // HALF_WIN
    n_halo = seq_len // HALF_WIN
    n_hb = B_HEADS // hb_heads
    merged = tuple(dd for dd, _, _ in partials)
    assert not merged or hb_heads == STAT_HEADS
    bb = jnp.asarray(_band_bias(d))

    def main(col0, width=hb_w):
        return pl.BlockSpec((None, None, tq, width),
                            lambda bi, r, hb, i: (bi, r, i, col0 // width + hb))

    def left(col0):
        return pl.BlockSpec((None, None, HALF_WIN, hb_w),
                            lambda bi, r, hb, i: (bi, r, jnp.maximum(i * halo_per_blk - 1, 0),
                                                  col0 // hb_w + hb))

    def right(col0):
        return pl.BlockSpec((None, None, HALF_WIN, hb_w),
                            lambda bi, r, hb, i: (bi, r, jnp.minimum((i + 1) * halo_per_blk, n_halo - 1),
                                                  col0 // hb_w + hb))

    def part(dd, width):
        return pl.BlockSpec((None, dd, tq // dd, width), lambda bi, r, hb, i: (bi, 0, i, hb))

    in_specs = [main(col_q), left(col_k), main(col_k), right(col_k),
                left(col_v), main(col_v), right(col_v),
                pl.BlockSpec((hb_heads, SUB, WIN), lambda bi, r, hb, i: (hb, 0, 0))]
    args = [src] * 7 + [bb]
    scratch = [pltpu.VMEM((tq + 2 * HALF_WIN, hb_w), BF16), pltpu.VMEM((tq + 2 * HALF_WIN, 2 * hb_w), BF16)]
    for dd, o_part, l_part in partials:
        in_specs += [part(dd, hb_w), part(dd, LANES)]
        args += [o_part, l_part]
        scratch += [pltpu.VMEM((hb_heads, tq, LANES), F32), pltpu.VMEM((tq, LANES), F32)]
    if merged:
        out_shape = jax.ShapeDtypeStruct((b, d, seq_len, D_B), BF16)
        out_specs = main(0)
    else:
        out_shape = (jax.ShapeDtypeStruct((b, d, seq_len, D_B), BF16),
                     jax.ShapeDtypeStruct((b, d, seq_len, B_HEADS // STAT_HEADS * LANES), F32))
        out_specs = (main(0), main(0, stat_w))
    return pl.pallas_call(
        functools.partial(_attn_b_kernel, merged=merged, seq_len=seq_len),
        out_shape=out_shape,
        grid=(b, d, n_hb, nblk),
        in_specs=in_specs,
        out_specs=out_specs,
        scratch_shapes=scratch,
        compiler_params=_cparams(("parallel", "parallel", "parallel", "arbitrary")),
        name=f"attn_b_d{d}",
    )(*args)


def _attn_b(proj3, strided):
    b, t, _ = proj3.shape
    assert all(w // (2 * d) == HALF_WIN for w, d in DILATED_PATTERNS) and DILATED_PATTERNS[0][1] == 1
    partials = []
    for d, src in zip(STRIDES, strided):
        o_part, l_part = _attn_b_pattern(src, (0, D_B, 2 * D_B), d, [], hb_heads=B_HEADS, tq=1024)
        partials.append((d, o_part, l_part))
    y = _attn_b_pattern(proj3.reshape(b, 1, t, D_IN), (COL_QB, COL_KB, COL_VB), 1, partials,
                        hb_heads=STAT_HEADS, tq=1024)
    return y.reshape(b, t, D_B)


OUT_CHUNK = 512


def _gated(y_ref, g_refs, gain_ref, rows):
    y = y_ref[rows, :].astype(F32)
    ms = jnp.mean(y * y, axis=-1, keepdims=True)
    yn = y * lax.rsqrt(ms + EPS) * gain_ref[...]
    silu_g = jnp.concatenate([r[rows, :] for r in g_refs], axis=-1).astype(F32)
    return (yn * silu_g).astype(BF16)


def _outproj_kernel(x_ref, ya_ref, yb_ref, ga0_ref, ga1_ref, gb0_ref, gb1_ref, na_ref, nb_ref,
                    w_ref, fn_ref, o_ref, *h_ref):
    for ch in range(x_ref.shape[0] // OUT_CHUNK):
        rows = slice(ch * OUT_CHUNK, (ch + 1) * OUT_CHUNK)
        za = _gated(ya_ref, (ga0_ref, ga1_ref), na_ref, rows)
        zb = _gated(yb_ref, (gb0_ref, gb1_ref), nb_ref, rows)
        out = x_ref[rows, :]
        out = out + jnp.dot(za, w_ref[0:D_A, :], preferred_element_type=F32)
        out = out + jnp.dot(zb, w_ref[D_A:, :], preferred_element_type=F32)
        ms = jnp.mean(out * out, axis=-1, keepdims=True)
        normed = out * lax.rsqrt(ms + EPS) * fn_ref[...]
        if h_ref:
            o_ref[rows, :] = out
            h_ref[0][rows, :] = normed.astype(h_ref[0].dtype)
        else:
            o_ref[rows, :] = normed


def _outproj(x2, ya2, yb2, proj2, na, nb, w, fn, *, layer, final, tm=512):
    m = x2.shape[0]
    half = D_A // 2

    def gate(col0, k):
        return pl.BlockSpec((tm, half), lambda i: (i, col0 // half + k))

    row = lambda width: pl.BlockSpec((tm, width), lambda i: (i, 0))
    const = lambda shape: pl.BlockSpec(shape, lambda i: (0, 0))
    stream = jax.ShapeDtypeStruct((m, D_MODEL), F32)
    return pl.pallas_call(
        _outproj_kernel,
        out_shape=stream if final else (stream, jax.ShapeDtypeStruct((m, D_MODEL), BF16)),
        grid=(m // tm,),
        in_specs=[row(D_MODEL), row(D_A), row(D_B),
                  gate(COL_GA, 0), gate(COL_GA, 1), gate(COL_GB, 0), gate(COL_GB, 1),
                  const((1, D_A)), const((1, D_B)),
                  pl.BlockSpec((None, D_A + D_B, D_MODEL), lambda i: (layer, 0, 0)), const((1, D_MODEL))],
        out_specs=row(D_MODEL) if final else (row(D_MODEL), row(D_MODEL)),
        compiler_params=_cparams(("parallel",)),
        name="outproj",
    )(x2, ya2, yb2, proj2, proj2, proj2, proj2, na, nb, w, fn)


def _rope_tables(t):
    pos = np.arange(t)
    row = (pos // GRID_W).astype(np.float64)
    col = (pos % GRID_W).astype(np.float64)
    inv_freq = ROPE_THETA ** (-np.arange(0, AXIS_DIM, 2, dtype=np.float64) / AXIS_DIM)
    ang_r = row[:, None] * inv_freq[None, :]
    ang_c = col[:, None] * inv_freq[None, :]
    cos = np.concatenate([np.cos(ang_r)] * 2 + [np.cos(ang_c)] * 2, axis=-1)
    sin = np.concatenate([-np.sin(ang_r), np.sin(ang_r), -np.sin(ang_c), np.sin(ang_c)], axis=-1)
    return jnp.asarray(cos, F32), jnp.asarray(sin, F32)


def kernel(x, norm_w, w_in, q_norm_a, k_norm_a, out_norm_a, out_norm_b, w_out, final_norm):
    b, t, d_model = x.shape
    depth = w_in.shape[0]
    assert d_model == D_MODEL and w_in.shape[1:] == (D_MODEL, D_IN)
    assert t % (DILATED_PATTERNS[-1][1] * SUB) == 0
    cos, sin = _rope_tables(t)
    col_scale = jnp.ones((D_IN,), F32).at[COL_QB:COL_KB].set(SCALE * LOG2E)
    w_in_bf = (w_in * col_scale).astype(BF16)
    w_out_bf = w_out.astype(BF16)
    x2 = x.reshape(b * t, D_MODEL)
    h2 = _norm(x2, norm_w[0][None])
    for l in range(depth):
        final = l == depth - 1
        proj2, *strided = _inproj(h2, w_in_bf, (q_norm_a[l] * (SCALE * LOG2E))[None],
                                  k_norm_a[l][None], cos, sin, layer=l, seq=t)
        proj3 = proj2.reshape(b, t, D_IN)
        ya = _attn_a(proj3)
        yb = _attn_b(proj3, strided)
        out = _outproj(x2, ya.reshape(b * t, D_A), yb.reshape(b * t, D_B), proj2,
                       out_norm_a[l][None], out_norm_b[l][None], w_out_bf,
                       (final_norm if final else norm_w[l + 1])[None], layer=l, final=final)
        x2, h2 = (out, None) if final else out
    return x2.reshape(b, t, D_MODEL)
```

```python
import functools

import numpy as np
import jax
import jax.numpy as jnp
from jax import lax
from jax.experimental import pallas as pl
from jax.experimental.pallas import tpu as pltpu

D_MODEL = 2048
HEAD_DIM = 128
A_HEADS = 8
A_KV_HEADS = 2
A_GROUP = A_HEADS // A_KV_HEADS
B_HEADS = 8
D_A = A_HEADS * HEAD_DIM
D_B = B_HEADS * HEAD_DIM
KV_DIM = A_KV_HEADS * HEAD_DIM
D_IN = 2 * D_A + 2 * KV_DIM + 4 * D_B
GRID_W = 64
AXIS_DIM = HEAD_DIM // 2
ROPE_THETA = 10000.0
DILATED_PATTERNS = ((128, 1), (512, 4), (2048, 16))
HALF_WIN = 64
SCALE = HEAD_DIM ** -0.5
LOG2E = float(np.log2(np.e))
EPS = 1e-6
NEG_INF = -1e30

COL_QA, COL_KA, COL_VA, COL_GA = 0, D_A, D_A + KV_DIM, D_A + 2 * KV_DIM
COL_QB = COL_GA + D_A
COL_KB, COL_VB, COL_GB = COL_QB + D_B, COL_QB + 2 * D_B, COL_QB + 3 * D_B

LANES = 128
VMEM_LIMIT = 56 * 1024 * 1024

BF16 = jnp.bfloat16
F32 = jnp.float32


def _cparams(sem):
    return pltpu.CompilerParams(dimension_semantics=sem, vmem_limit_bytes=VMEM_LIMIT)


def _rope_group(a, gain, cos, sin):
    ms = jnp.mean(a * a, axis=-1, keepdims=True)
    y = a * lax.rsqrt(ms + EPS) * gain
    lane = lax.broadcasted_iota(jnp.int32, y.shape, 1)
    partner = jnp.where((lane % AXIS_DIM) < AXIS_DIM // 2,
                        pltpu.roll(y, LANES - AXIS_DIM // 2, 1), pltpu.roll(y, AXIS_DIM // 2, 1))
    return y * cos + partner * sin


STRIDES = tuple(d for _, d in DILATED_PATTERNS if d > 1)
QKVB_W = COL_GB - COL_QB
ROW_CHUNK = 256
INPROJ_TN = 512
N_SLABS = 2


def _inproj_kernel(h_ref, w_ref, qg_ref, kg_ref, cos_ref, sin_ref, o_ref, *rest, tn):
    strided_refs, (slab_ref, slab2_ref) = rest[:len(STRIDES)], rest[len(STRIDES):]
    j = pl.program_id(1)
    tm = h_ref.shape[0]
    n_chunks = tm // ROW_CHUNK
    groups_per_tile = tn // HEAD_DIM
    first_b, end_b = COL_QB // tn, COL_GB // tn

    def tile_body(kinds, strided):
        for ch in range(n_chunks):
            rows = slice(ch * ROW_CHUNK, (ch + 1) * ROW_CHUNK)
            acc = jnp.dot(h_ref[rows, :], w_ref[...], preferred_element_type=F32)
            for c, kind in enumerate(kinds):
                a = acc[:, c * HEAD_DIM:(c + 1) * HEAD_DIM]
                if kind == 'g':
                    a = a * jax.nn.sigmoid(a)
                elif kind is not None:
                    gain = qg_ref[...] if kind == 'q' else kg_ref[...]
                    a = _rope_group(a, gain, cos_ref[rows, :], sin_ref[rows, :])
                o_ref[rows, c * HEAD_DIM:(c + 1) * HEAD_DIM] = a.astype(o_ref.dtype)
                if strided:
                    slab_ref[ch % N_SLABS, c] = a
            if strided:
                (d1, ref1), (d2, ref2) = zip(STRIDES, strided_refs)
                n1, n2, step = ROW_CHUNK // d1, ROW_CHUNK // d2, d2 // d1
                for c in range(groups_per_tile):
                    cols = slice(c * LANES, (c + 1) * LANES)
                    for r1 in range(d1):
                        v = slab_ref[ch % N_SLABS, c, pl.ds(r1, n1, stride=d1), :]
                        ref1[r1, ch * n1:(ch + 1) * n1, cols] = v.astype(ref1.dtype)
                        slab2_ref[ch % N_SLABS, c, r1] = v
                    for r1 in range(d1):
                        for q2 in range(step):
                            v = slab2_ref[ch % N_SLABS, c, r1, pl.ds(q2, n2, stride=step), :]
                            ref2[q2 * d1 + r1, ch * n2:(ch + 1) * n2, cols] = v.astype(ref2.dtype)

    def kind_of(col):
        if col < COL_KA:
            return 'q'
        if col < COL_VA:
            return 'k'
        if COL_GA <= col < COL_QB or col >= COL_GB:
            return 'g'
        return None

    variants = {}
    for tile in range(D_IN // tn):
        kinds = tuple(kind_of(tile * tn + c * HEAD_DIM) for c in range(groups_per_tile))
        variants.setdefault((kinds, first_b <= tile < end_b), []).append(tile)
    for (kinds, strided), tiles in variants.items():
        cond = functools.reduce(jnp.logical_or, [j == t for t in tiles])
        pl.when(cond)(functools.partial(tile_body, kinds, strided))


def _norm_kernel(x_ref, nw_ref, h_ref):
    x = x_ref[...]
    ms = jnp.mean(x * x, axis=-1, keepdims=True)
    h_ref[...] = (x * lax.rsqrt(ms + EPS) * nw_ref[...]).astype(h_ref.dtype)


def _norm(x2, nw, *, tm=1024):
    m = x2.shape[0]
    return pl.pallas_call(
        _norm_kernel,
        out_shape=jax.ShapeDtypeStruct((m, D_MODEL), BF16),
        grid=(m // tm,),
        in_specs=[pl.BlockSpec((tm, D_MODEL), lambda i: (i, 0)), pl.BlockSpec((1, D_MODEL), lambda i: (0, 0))],
        out_specs=pl.BlockSpec((tm, D_MODEL), lambda i: (i, 0)),
        compiler_params=_cparams(("parallel",)),
        name="norm",
    )(x2, nw)


def _inproj(h2, w, qg, kg, cos, sin, *, layer, seq, tm=2048, tn=INPROJ_TN):
    m = h2.shape[0]
    tm = min(tm, seq)
    t_blocks = seq // tm
    batch = m // seq
    assert COL_QB % tn == 0 and COL_GB % tn == 0
    assert len(STRIDES) == 2 and STRIDES[1] % STRIDES[0] == 0 and ROW_CHUNK % STRIDES[1] == 0
    first_b, n_b = COL_QB // tn, QKVB_W // tn

    def strided_spec(d):
        return pl.BlockSpec((None, d, tm // d, tn),
                            lambda i, j: (i // t_blocks, 0, i % t_blocks, jnp.clip(j - first_b, 0, n_b - 1)))

    return pl.pallas_call(
        functools.partial(_inproj_kernel, tn=tn),
        out_shape=(jax.ShapeDtypeStruct((m, D_IN), BF16),)
        + tuple(jax.ShapeDtypeStruct((batch, d, seq // d, QKVB_W), BF16) for d in STRIDES),
        grid=(m // tm, D_IN // tn),
        in_specs=[
            pl.BlockSpec((tm, D_MODEL), lambda i, j: (i, 0)),
            pl.BlockSpec((None, D_MODEL, tn), lambda i, j: (layer, 0, j)),
            pl.BlockSpec((1, HEAD_DIM), lambda i, j: (0, 0)),
            pl.BlockSpec((1, HEAD_DIM), lambda i, j: (0, 0)),
            pl.BlockSpec((tm, HEAD_DIM), lambda i, j: (i % t_blocks, 0)),
            pl.BlockSpec((tm, HEAD_DIM), lambda i, j: (i % t_blocks, 0)),
        ],
        out_specs=(pl.BlockSpec((tm, tn), lambda i, j: (i, j)),) + tuple(strided_spec(d) for d in STRIDES),
        scratch_shapes=[pltpu.VMEM((N_SLABS, tn // LANES, ROW_CHUNK, LANES), F32),
                        pltpu.VMEM((N_SLABS, tn // LANES, STRIDES[0], ROW_CHUNK // STRIDES[0], LANES), F32)],
        compiler_params=_cparams(("arbitrary", "arbitrary")),
        name="inproj",
    )(h2, w, qg, kg, cos, sin)


ONES_ROWS = 16
KV_UNROLL = 4


def _attn_a_kernel(q_ref, k_ref, v_ref, o_ref, vt_ref, s0_ref, s1_ref, acc_ref, qt_ref, *, tk, tq):
    n_q = q_ref.shape[0] // tq
    n_kv = k_ref.shape[0] // tk
    assert n_kv % 2 == 0

    @pl.when(pl.program_id(2) == 0)
    def _():
        for kk in range(n_kv):
            vt_ref[kk, 0:HEAD_DIM, :] = v_ref[kk * tk:(kk + 1) * tk, :].astype(F32).T.astype(BF16)
            vt_ref[kk, HEAD_DIM:, :] = jnp.ones((ONES_ROWS, tk), BF16)

    sbuf = (s0_ref, s1_ref)
    for qi in range(n_q):
        for hh in range(A_GROUP):
            qt_ref[qi, hh] = (q_ref[qi * tq:(qi + 1) * tq, hh * HEAD_DIM:(hh + 1) * HEAD_DIM]
                              .astype(F32).T.astype(BF16))

    def scores(qi, kk, dst):
        start = pl.multiple_of(kk * tk, tk)
        k = k_ref[pl.ds(start, tk), :]
        for hh in range(A_GROUP):
            dst[hh] = jnp.dot(k, qt_ref[qi, hh], preferred_element_type=F32)

    def softmax_pv(kk, src, ml):
        vt = vt_ref[kk]
        new = []
        for hh in range(A_GROUP):
            m, l = ml[hh]
            s = src[hh]
            m_new = jnp.maximum(m, jnp.max(s, axis=0, keepdims=True))
            alpha = jnp.exp2(m - m_new)
            p = jnp.exp2(s - m_new).astype(BF16)
            pv = jnp.dot(vt, p, preferred_element_type=F32)
            acc_ref[hh] = alpha * acc_ref[hh] + pv[0:HEAD_DIM]
            new.append((m_new, alpha * l + pv[HEAD_DIM:HEAD_DIM + 1]))
        return tuple(new)

    unroll = min(KV_UNROLL, n_kv)

    def trip(qi, kk0, ml, last):
        for u in range(unroll):
            if last and u == unroll - 1:
                scores(jnp.minimum(qi + 1, n_q - 1), 0, sbuf[(u + 1) % 2])
            else:
                scores(qi, kk0 + u + 1, sbuf[(u + 1) % 2])
            ml = softmax_pv(kk0 + u, sbuf[u % 2], ml)
        return ml

    def query_tile(qi, carry):
        acc_ref[...] = jnp.zeros_like(acc_ref)
        ml = tuple((jnp.full((1, tq), -jnp.inf, F32), jnp.zeros((1, tq), F32)) for _ in range(A_GROUP))
        ml = lax.fori_loop(0, n_kv // unroll - 1, lambda jj, ml: trip(qi, jj * unroll, ml, False), ml)
        ml = trip(qi, n_kv - unroll, ml, True)
        rows = pl.ds(pl.multiple_of(qi * tq, tq), tq)
        for hh in range(A_GROUP):
            o_ref[rows, hh * HEAD_DIM:(hh + 1) * HEAD_DIM] = (acc_ref[hh] / ml[hh][1]).T.astype(o_ref.dtype)
        return carry

    scores(0, 0, sbuf[0])
    lax.fori_loop(0, n_q, query_tile, 0)


def _attn_a(proj3, *, tq=256, n_q=4, tk=512):
    b, t, _ = proj3.shape
    gw = A_GROUP * HEAD_DIM
    rows = n_q * tq
    return pl.pallas_call(
        functools.partial(_attn_a_kernel, tk=tk, tq=tq),
        out_shape=jax.ShapeDtypeStruct((b, t, D_A), BF16),
        grid=(b, A_KV_HEADS, t // rows),
        in_specs=[
            pl.BlockSpec((None, rows, gw), lambda bi, g, qi: (bi, qi, COL_QA // gw + g)),
            pl.BlockSpec((None, t, HEAD_DIM), lambda bi, g, qi: (bi, 0, COL_KA // HEAD_DIM + g)),
            pl.BlockSpec((None, t, HEAD_DIM), lambda bi, g, qi: (bi, 0, COL_VA // HEAD_DIM + g)),
        ],
        out_specs=pl.BlockSpec((None, rows, gw), lambda bi, g, qi: (bi, qi, g)),
        scratch_shapes=[pltpu.VMEM((t // tk, HEAD_DIM + ONES_ROWS, tk), BF16),
                        pltpu.VMEM((A_GROUP, tk, tq), F32), pltpu.VMEM((A_GROUP, tk, tq), F32),
                        pltpu.VMEM((A_GROUP, HEAD_DIM, tq), F32),
                        pltpu.VMEM((n_q, A_GROUP, HEAD_DIM, tq), BF16)],
        compiler_params=_cparams(("parallel", "parallel", "arbitrary")),
        name="attn_a",
    )(proj3, proj3, proj3)


SUB = 128
WIN = SUB + 2 * HALF_WIN
STAT_HEADS = 4
LSE_REP = LANES // STAT_HEADS


def _attn_b_kernel(*refs, merged, seq_len):
    it = iter(refs)
    q_ref, kl_ref, km_ref, kr_ref, vl_ref, vm_ref, vr_ref, bb_ref = (next(it) for _ in range(8))
    parts = [(next(it), next(it)) for _ in merged]
    o_ref = next(it)
    l_ref = None if merged else next(it)
    kbuf, vbuf = next(it), next(it)
    flat = [(next(it), next(it)) for _ in merged]

    tq = q_ref.shape[0]
    n_heads = q_ref.shape[1] // HEAD_DIM
    n_sub = tq // SUB
    i = pl.program_id(3)
    kbuf[0:HALF_WIN, :] = kl_ref[...]
    kbuf[HALF_WIN:HALF_WIN + tq, :] = km_ref[...]
    kbuf[HALF_WIN + tq:, :] = kr_ref[...]
    ones = jnp.ones((tq + 2 * HALF_WIN, HEAD_DIM), BF16)
    for h in range(n_heads):
        c0, w0 = h * HEAD_DIM, 2 * h * HEAD_DIM
        vbuf[0:HALF_WIN, w0:w0 + HEAD_DIM] = vl_ref[:, c0:c0 + HEAD_DIM]
        vbuf[HALF_WIN:HALF_WIN + tq, w0:w0 + HEAD_DIM] = vm_ref[:, c0:c0 + HEAD_DIM]
        vbuf[HALF_WIN + tq:, w0:w0 + HEAD_DIM] = vr_ref[:, c0:c0 + HEAD_DIM]
        vbuf[:, w0 + HEAD_DIM:w0 + 2 * HEAD_DIM] = ones

    for d, (po_ref, pl_ref), (fo_ref, fl_ref) in zip(merged, parts, flat):
        for r in range(d):
            fl_ref[pl.ds(r, tq // d, stride=d), :] = pl_ref[r]
            for h in range(n_heads):
                fo_ref[h, pl.ds(r, tq // d, stride=d), :] = (
                    po_ref[r, :, h * HEAD_DIM:(h + 1) * HEAD_DIM].astype(F32))

    col = lax.broadcasted_iota(jnp.int32, (1, WIN), 1)
    lane_head = lax.broadcasted_iota(jnp.int32, (SUB, LANES), 1) // LSE_REP
    for sub in range(n_sub):
        r0 = sub * SUB
        kidx = i * tq + (r0 - HALF_WIN) + col
        colmask = jnp.where((kidx >= 0) & (kidx < seq_len), 0.0, NEG_INF).astype(F32)
        stats = [jnp.zeros((SUB, LANES), F32)] * (n_heads // STAT_HEADS)
        for h in range(n_heads):
            c0 = h * HEAD_DIM
            q = q_ref[r0:r0 + SUB, c0:c0 + HEAD_DIM]
            k = kbuf[r0:r0 + WIN, c0:c0 + HEAD_DIM]
            s = lax.dot_general(q, k, (((1,), (1,)), ((), ())), preferred_element_type=F32)
            s = s + bb_ref[h]
            if sub == 0 or sub == n_sub - 1:
                s = s + colmask
            m = jnp.max(s, axis=-1, keepdims=True)
            p = jnp.exp2(s - m).astype(BF16)
            pv = jnp.dot(p, vbuf[r0:r0 + WIN, 2 * c0:2 * c0 + 2 * HEAD_DIM], preferred_element_type=F32)
            l = pv[:, HEAD_DIM:]
            o = pv[:, 0:HEAD_DIM] / l
            lse = m + jnp.log2(l)
            if merged:
                lses = [lse] + [fl_ref[r0:r0 + SUB, h * LSE_REP:h * LSE_REP + 1] for _, fl_ref in flat]
                outs = [o] + [fo_ref[h, r0:r0 + SUB, :] for fo_ref, _ in flat]
                top = functools.reduce(jnp.maximum, lses)
                ws = [jnp.exp2(x - top) for x in lses]
                o = sum(w * x for w, x in zip(ws, outs)) / sum(ws)
            else:
                st = h // STAT_HEADS
                stats[st] = jnp.where(lane_head == h % STAT_HEADS, lse, stats[st])
            o_ref[r0:r0 + SUB, c0:c0 + HEAD_DIM] = o.astype(o_ref.dtype)
        if not merged:
            for st, tile in enumerate(stats):
                l_ref[r0:r0 + SUB, st * LANES:(st + 1) * LANES] = tile


def _band_bias(d):
    a = np.arange(SUB)[:, None]
    c = np.arange(WIN)[None, :]
    off = c - HALF_WIN - a
    slopes = (2.0 ** -np.arange(1, B_HEADS + 1)).astype(np.float32)
    bias = -slopes[:, None, None] * (np.abs(off) * d).astype(np.float32)[None] * np.float32(LOG2E)
    return np.where((np.abs(off) <= HALF_WIN)[None], bias, np.float32(NEG_INF)).astype(np.float32)


def _attn_b_pattern(src, cols, d, partials, *, hb_heads, tq):
    b, _, seq_len, _ = src.shape
    col_q, col_k, col_v = cols
    tq = min(tq, seq_len)
    hb_w = hb_heads * HEAD_DIM
    stat_w = hb_heads // STAT_HEADS * LANES
    assert all(c % hb_w == 0 for c in cols) and hb_heads % STAT_HEADS == 0
    nblk = seq_len // tq
    halo_per_blk = tq // HALF_WIN
    n_halo = seq_len // HALF_WIN
    n_hb = B_HEADS // hb_heads
    merged = tuple(dd for dd, _, _ in partials)
    assert not merged or hb_heads == STAT_HEADS
    bb = jnp.asarray(_band_bias(d))

    def main(col0, width=hb_w):
        return pl.BlockSpec((None, None, tq, width),
                            lambda bi, r, hb, i: (bi, r, i, col0 // width + hb))

    def left(col0):
        return pl.BlockSpec((None, None, HALF_WIN, hb_w),
                            lambda bi, r, hb, i: (bi, r, jnp.maximum(i * halo_per_blk - 1, 0),
                                                  col0 // hb_w + hb))

    def right(col0):
        return pl.BlockSpec((None, None, HALF_WIN, hb_w),
                            lambda bi, r, hb, i: (bi, r, jnp.minimum((i + 1) * halo_per_blk, n_halo - 1),
                                                  col0 // hb_w + hb))

    def part(dd, width):
        return pl.BlockSpec((None, dd, tq // dd, width), lambda bi, r, hb, i: (bi, 0, i, hb))

    in_specs = [main(col_q), left(col_k), main(col_k), right(col_k),
                left(col_v), main(col_v), right(col_v),
                pl.BlockSpec((hb_heads, SUB, WIN), lambda bi, r, hb, i: (hb, 0, 0))]
    args = [src] * 7 + [bb]
    scratch = [pltpu.VMEM((tq + 2 * HALF_WIN, hb_w), BF16), pltpu.VMEM((tq + 2 * HALF_WIN, 2 * hb_w), BF16)]
    for dd, o_part, l_part in partials:
        in_specs += [part(dd, hb_w), part(dd, LANES)]
        args += [o_part, l_part]
        scratch += [pltpu.VMEM((hb_heads, tq, LANES), F32), pltpu.VMEM((tq, LANES), F32)]
    if merged:
        out_shape = jax.ShapeDtypeStruct((b, d, seq_len, D_B), BF16)
        out_specs = main(0)
    else:
        out_shape = (jax.ShapeDtypeStruct((b, d, seq_len, D_B), BF16),
                     jax.ShapeDtypeStruct((b, d, seq_len, B_HEADS // STAT_HEADS * LANES), F32))
        out_specs = (main(0), main(0, stat_w))
    return pl.pallas_call(
        functools.partial(_attn_b_kernel, merged=merged, seq_len=seq_len),
        out_shape=out_shape,
        grid=(b, d, n_hb, nblk),
        in_specs=in_specs,
        out_specs=out_specs,
        scratch_shapes=scratch,
        compiler_params=_cparams(("parallel", "parallel", "parallel", "arbitrary")),
        name=f"attn_b_d{d}",
    )(*args)


def _attn_b(proj3, strided):
    b, t, _ = proj3.shape
    assert all(w // (2 * d) == HALF_WIN for w, d in DILATED_PATTERNS) and DILATED_PATTERNS[0][1] == 1
    partials = []
    for d, src in zip(STRIDES, strided):
        o_part, l_part = _attn_b_pattern(src, (0, D_B, 2 * D_B), d, [], hb_heads=B_HEADS, tq=1024)
        partials.append((d, o_part, l_part))
    y = _attn_b_pattern(proj3.reshape(b, 1, t, D_IN), (COL_QB, COL_KB, COL_VB), 1, partials,
                        hb_heads=STAT_HEADS, tq=2048)
    return y.reshape(b, t, D_B)


OUT_CHUNK = 512


def _gated(y_ref, g_refs, gain_ref, rows):
    y = y_ref[rows, :].astype(F32)
    ms = jnp.mean(y * y, axis=-1, keepdims=True)
    yn = y * lax.rsqrt(ms + EPS) * gain_ref[...]
    silu_g = jnp.concatenate([r[rows, :] for r in g_refs], axis=-1).astype(F32)
    return (yn * silu_g).astype(BF16)


def _outproj_kernel(x_ref, ya_ref, yb_ref, ga0_ref, ga1_ref, gb0_ref, gb1_ref, na_ref, nb_ref,
                    w_ref, fn_ref, o_ref, *h_ref):
    for ch in range(x_ref.shape[0] // OUT_CHUNK):
        rows = slice(ch * OUT_CHUNK, (ch + 1) * OUT_CHUNK)
        za = _gated(ya_ref, (ga0_ref, ga1_ref), na_ref, rows)
        zb = _gated(yb_ref, (gb0_ref, gb1_ref), nb_ref, rows)
        out = x_ref[rows, :]
        out = out + jnp.dot(za, w_ref[0:D_A, :], preferred_element_type=F32)
        out = out + jnp.dot(zb, w_ref[D_A:, :], preferred_element_type=F32)
        ms = jnp.mean(out * out, axis=-1, keepdims=True)
        normed = out * lax.rsqrt(ms + EPS) * fn_ref[...]
        if h_ref:
            o_ref[rows, :] = out
            h_ref[0][rows, :] = normed.astype(h_ref[0].dtype)
        else:
            o_ref[rows, :] = normed


def _outproj(x2, ya2, yb2, proj2, na, nb, w, fn, *, layer, final, tm=512):
    m = x2.shape[0]
    half = D_A // 2

    def gate(col0, k):
        return pl.BlockSpec((tm, half), lambda i: (i, col0 // half + k))

    row = lambda width: pl.BlockSpec((tm, width), lambda i: (i, 0))
    const = lambda shape: pl.BlockSpec(shape, lambda i: (0, 0))
    stream = jax.ShapeDtypeStruct((m, D_MODEL), F32)
    return pl.pallas_call(
        _outproj_kernel,
        out_shape=stream if final else (stream, jax.ShapeDtypeStruct((m, D_MODEL), BF16)),
        grid=(m // tm,),
        in_specs=[row(D_MODEL), row(D_A), row(D_B),
                  gate(COL_GA, 0), gate(COL_GA, 1), gate(COL_GB, 0), gate(COL_GB, 1),
                  const((1, D_A)), const((1, D_B)),
                  pl.BlockSpec((None, D_A + D_B, D_MODEL), lambda i: (layer, 0, 0)), const((1, D_MODEL))],
        out_specs=row(D_MODEL) if final else (row(D_MODEL), row(D_MODEL)),
        compiler_params=_cparams(("parallel",)),
        name="outproj",
    )(x2, ya2, yb2, proj2, proj2, proj2, proj2, na, nb, w, fn)


def _rope_tables(t):
    pos = np.arange(t)
    row = (pos // GRID_W).astype(np.float64)
    col = (pos % GRID_W).astype(np.float64)
    inv_freq = ROPE_THETA ** (-np.arange(0, AXIS_DIM, 2, dtype=np.float64) / AXIS_DIM)
    ang_r = row[:, None] * inv_freq[None, :]
    ang_c = col[:, None] * inv_freq[None, :]
    cos = np.concatenate([np.cos(ang_r)] * 2 + [np.cos(ang_c)] * 2, axis=-1)
    sin = np.concatenate([-np.sin(ang_r), np.sin(ang_r), -np.sin(ang_c), np.sin(ang_c)], axis=-1)
    return jnp.asarray(cos, F32), jnp.asarray(sin, F32)


def kernel(x, norm_w, w_in, q_norm_a, k_norm_a, out_norm_a, out_norm_b, w_out, final_norm):
    b, t, d_model = x.shape
    depth = w_in.shape[0]
    assert d_model == D_MODEL and w_in.shape[1:] == (D_MODEL, D_IN)
    assert t % (DILATED_PATTERNS[-1][1] * SUB) == 0
    cos, sin = _rope_tables(t)
    col_scale = jnp.ones((D_IN,), F32).at[COL_QB:COL_KB].set(SCALE * LOG2E)
    w_in_bf = (w_in * col_scale).astype(BF16)
    w_out_bf = w_out.astype(BF16)
    x2 = x.reshape(b * t, D_MODEL)
    h2 = _norm(x2, norm_w[0][None])
    for l in range(depth):
        final = l == depth - 1
        proj2, *strided = _inproj(h2, w_in_bf, (q_norm_a[l] * (SCALE * LOG2E))[None],
                                  k_norm_a[l][None], cos, sin, layer=l, seq=t)
        proj3 = proj2.reshape(b, t, D_IN)
        ya = _attn_a(proj3)
        yb = _attn_b(proj3, strided)
        out = _outproj(x2, ya.reshape(b * t, D_A), yb.reshape(b * t, D_B), proj2,
                       out_norm_a[l][None], out_norm_b[l][None], w_out_bf,
                       (final_norm if final else norm_w[l + 1])[None], layer=l, final=final)
        x2, h2 = (out, None) if final else out
    return x2.reshape(b, t, D_MODEL)
```
